```python
import math
import jax, jax.numpy as jnp
from jax import lax
import numpy as np

D_MODEL = 2048
BATCH = 8
SEQ = 4096
DEPTH = 1
DEC_BATCH = 4
DEC_SEQ = 8192
PAST_LEN = 128

MIX_WIDTH = D_MODEL
D_A = MIX_WIDTH // 2
HD_A = 128
H_A = D_A // HD_A
D_B = MIX_WIDTH - D_A
HD_B = 128
H_B = D_B // HD_B
CHUNK = 128
D_IN = D_A + 2 * D_B
N_EXPERTS = 32
TOP_K = 4
D_FF = D_MODEL
SWIGLU_LIMIT = 7.0
SWIGLU_ALPHA = 1.702
BLK = 256
LN_EPS = 1e-5
ALPHA = (2.0 * DEPTH) ** 0.25
BETA = (8.0 * DEPTH) ** -0.25

kernel_name = "hymba_fnet_gmlp_moe_deepnorm_encoder"


def layer_norm(x, g, b):
    xf = x.astype(jnp.float32)
    mu = jnp.mean(xf, axis=-1, keepdims=True)
    xc = xf - mu
    var = jnp.mean(jnp.square(xc), axis=-1, keepdims=True)
    return (xc * lax.rsqrt(var + LN_EPS) * g.astype(jnp.float32) + b.astype(jnp.float32)).astype(x.dtype)


def token_mixer(h, w_in, sgu_norm_g, sgu_norm_b, sgu_w, sgu_b, w_o):
    bn, s, _ = h.shape
    z = h @ w_in
    za = z[..., :D_A].reshape(bn, s, H_A, HD_A).astype(jnp.float32)
    ya = jnp.fft.fft2(za, axes=(1, 3), norm="ortho").real.astype(h.dtype).reshape(bn, s, D_A)
    zb = jax.nn.gelu(z[..., D_A:], approximate=False)
    u = zb[..., :D_B]
    v = layer_norm(zb[..., D_B:].reshape(bn, s, H_B, HD_B),
                   sgu_norm_g.reshape(H_B, HD_B), sgu_norm_b.reshape(H_B, HD_B))
    v = v.reshape(bn, s // CHUNK, CHUNK, H_B, HD_B)
    gate = jnp.einsum('hpq,bnqhc->bnphc', sgu_w, v) + sgu_b.T[:, :, None]
    yb = u * gate.reshape(bn, s, D_B)
    return jnp.concatenate([ya, yb], axis=-1) @ w_o


def clamped_swiglu(z):
    x_glu = jnp.minimum(z[..., ::2], SWIGLU_LIMIT)
    x_lin = jnp.clip(z[..., 1::2], -SWIGLU_LIMIT, SWIGLU_LIMIT)
    return x_glu * jax.nn.sigmoid(SWIGLU_ALPHA * x_glu) * (x_lin + 1)


def moe(h, router_w, router_b, w_gu, b_gu, w_down, b_down):
    t, d = h.shape
    logits = (h @ router_w + router_b).astype(jnp.float32)
    top_vals, top_idx = lax.top_k(logits, TOP_K)
    gates = jax.nn.softmax(top_vals, axis=-1).astype(h.dtype)
    tk = t * TOP_K
    flat_e = top_idx.reshape(-1)
    flat_tok = jnp.arange(tk, dtype=jnp.int32) // TOP_K
    flat_g = gates.reshape(-1)
    order = jnp.argsort(flat_e)
    se = flat_e[order]
    counts = jnp.bincount(flat_e, length=N_EXPERTS)
    starts = jnp.cumsum(counts) - counts
    padded = ((counts + BLK - 1) // BLK) * BLK
    pends = jnp.cumsum(padded)
    pstarts = pends - padded
    dest = pstarts[se] + jnp.arange(tk, dtype=jnp.int32) - starts[se]
    n_pad = ((tk + BLK - 1) // BLK) * BLK + N_EXPERTS * BLK
    n_blocks = n_pad // BLK
    src = jnp.full((n_pad,), t, dtype=jnp.int32).at[dest].set(flat_tok[order])
    gbuf = jnp.zeros((n_pad,), h.dtype).at[dest].set(flat_g[order])
    blk_e = jnp.minimum(jnp.searchsorted(pends, jnp.arange(n_blocks) * BLK, side='right'),
                        N_EXPERTS - 1).astype(jnp.int32)
    hp = jnp.concatenate([h, jnp.zeros((1, d), h.dtype)], axis=0)
    xin = hp[src].reshape(n_blocks, BLK, d)

    def expert_block(args):
        xb, e = args
        a = clamped_swiglu(xb @ w_gu[e] + b_gu[e])
        return a @ w_down[e] + b_down[e]

    yb = lax.map(expert_block, (xin, blk_e)).reshape(n_pad, d)
    y = jax.ops.segment_sum(yb * gbuf[:, None], src, num_segments=t + 1)
    return y[:t]


def encoder(x, ln_in_g, ln_in_b, w_in, sgu_norm_g, sgu_norm_b, sgu_w, sgu_b, w_o,
            ln1_g, ln1_b, router_w, router_b, w_gu, b_gu, w_down, b_down, ln2_g, ln2_b):
    bn, s, d = x.shape
    x = layer_norm(x, ln_in_g, ln_in_b)
    for l in range(DEPTH):
        mix = token_mixer(x, w_in[l], sgu_norm_g[l], sgu_norm_b[l], sgu_w[l], sgu_b[l], w_o[l])
        x = layer_norm(ALPHA * x + mix, ln1_g[l], ln1_b[l])
        ff = moe(x.reshape(bn * s, d), router_w[l], router_b[l], w_gu[l], b_gu[l],
                 w_down[l], b_down[l]).reshape(bn, s, d)
        x = layer_norm(ALPHA * x + ff, ln2_g[l], ln2_b[l])
    return x


def setup_inputs(seed: int = 0) -> dict:
    key = jax.random.key(seed)
    ks = jax.random.split(key, 20)
    f32 = jnp.float32
    nrm = lambda k, shape, sc: jax.random.normal(k, shape, f32) * sc
    return {
        "x_prompt": nrm(ks[0], (BATCH, SEQ, D_MODEL), 1.0),
        "x_sample": nrm(ks[1], (DEC_BATCH, DEC_SEQ, D_MODEL), 1.0),
        "ln_in_g": 1.0 + nrm(ks[2], (D_MODEL,), 0.02),
        "ln_in_b": nrm(ks[3], (D_MODEL,), 0.02),
        "w_in": nrm(ks[4], (DEPTH, D_MODEL, D_IN), BETA * D_MODEL ** -0.5),
        "sgu_norm_g": 1.0 + nrm(ks[5], (DEPTH, D_B), 0.02),
        "sgu_norm_b": nrm(ks[6], (DEPTH, D_B), 0.02),
        "sgu_w": nrm(ks[7], (DEPTH, H_B, CHUNK, CHUNK), 0.5 * CHUNK ** -0.5),
        "sgu_b": 1.0 + nrm(ks[8], (DEPTH, H_B, CHUNK), 0.02),
        "w_o": nrm(ks[9], (DEPTH, MIX_WIDTH, D_MODEL), BETA * MIX_WIDTH ** -0.5),
        "ln1_g": 1.0 + nrm(ks[10], (DEPTH, D_MODEL), 0.02),
        "ln1_b": nrm(ks[11], (DEPTH, D_MODEL), 0.02),
        "router_w": nrm(ks[12], (DEPTH, D_MODEL, N_EXPERTS), D_MODEL ** -0.5),
        "router_b": nrm(ks[13], (DEPTH, N_EXPERTS), 0.01),
        "w_gu": nrm(ks[14], (DEPTH, N_EXPERTS, D_MODEL, 2 * D_FF), BETA * D_MODEL ** -0.5),
        "b_gu": nrm(ks[15], (DEPTH, N_EXPERTS, 2 * D_FF), 0.01),
        "w_down": nrm(ks[16], (DEPTH, N_EXPERTS, D_FF, D_MODEL), BETA * D_FF ** -0.5),
        "b_down": nrm(ks[17], (DEPTH, N_EXPERTS, D_MODEL), 0.01),
        "ln2_g": 1.0 + nrm(ks[18], (DEPTH, D_MODEL), 0.02),
        "ln2_b": nrm(ks[19], (DEPTH, D_MODEL), 0.02),
    }


def reference(x_prompt, x_sample, ln_in_g, ln_in_b, w_in, sgu_norm_g, sgu_norm_b, sgu_w, sgu_b,
              w_o, ln1_g, ln1_b, router_w, router_b, w_gu, b_gu, w_down, b_down, ln2_g, ln2_b):
    y_prompt = encoder(x_prompt, ln_in_g, ln_in_b, w_in, sgu_norm_g, sgu_norm_b, sgu_w, sgu_b, w_o,
                       ln1_g, ln1_b, router_w, router_b, w_gu, b_gu, w_down, b_down, ln2_g, ln2_b)
    y_sample = encoder(x_sample, ln_in_g, ln_in_b, w_in, sgu_norm_g, sgu_norm_b, sgu_w, sgu_b, w_o,
                       ln1_g, ln1_b, router_w, router_b, w_gu, b_gu, w_down, b_down, ln2_g, ln2_b)
    return (y_prompt, y_sample)
```

```python
import functools
import math

import jax
import jax.numpy as jnp
import numpy as np
from jax import lax
from jax.experimental import pallas as pl
from jax.experimental.pallas import tpu as pltpu

F32 = jnp.float32
BF16 = jnp.bfloat16

LN_EPS = 1e-5
TOP_K = 4
HEAD = 128
SWIGLU_LIMIT = 7.0
SWIGLU_ALPHA = 1.702
VMEM_LIMIT_BYTES = 56 * 1024 * 1024

ROW_TILE = 256
DFT_K_TILE = 256
DFT_S_TILE = 512
MOE_ROWS = 512
MOE_FF_TILE = 512


def _layer_norm(x, g, b):
    mu = jnp.mean(x, axis=-1, keepdims=True)
    xc = x - mu
    var = jnp.mean(xc * xc, axis=-1, keepdims=True)
    return xc * lax.rsqrt(var + LN_EPS) * g + b


def _gelu(z):
    return 0.5 * z * (1.0 + lax.erf(z * (1.0 / math.sqrt(2.0))))


def _dot(a, b):
    return jnp.dot(a, b, preferred_element_type=F32)


def _params(semantics):
    return pltpu.CompilerParams(dimension_semantics=semantics,
                                vmem_limit_bytes=VMEM_LIMIT_BYTES)


def _whole(shape):
    return pl.BlockSpec(shape, lambda *_: (0,) * len(shape),
                        pipeline_mode=pl.Buffered(1))


def _mix_in_body(x_ref, g_ref, b_ref, win_ref, cs_ref, sg_ref, sb_ref, sw_ref, sbias_ref,
                 a_ref, q_ref, yb_ref, *, d_a, d_b):
    xb = _layer_norm(x_ref[...], g_ref[...], b_ref[...]).astype(BF16)
    rows = xb.shape[0]

    za = _dot(xb, win_ref[:, :d_a]).astype(BF16)
    for h in range(d_a // HEAD):
        hs = slice(h * HEAD, (h + 1) * HEAD)
        ab = _dot(za[:, hs], cs_ref[...])
        a_ref[:, hs] = ab[:, :HEAD].astype(BF16)
        q_ref[:, hs] = ab[:, HEAD:].astype(BF16)

    u = _gelu(_dot(xb, win_ref[:, d_a:d_a + d_b]))
    v = _gelu(_dot(xb, win_ref[:, d_a + d_b:]))
    for h in range(d_b // HEAD):
        hs = slice(h * HEAD, (h + 1) * HEAD)
        vn = _layer_norm(v[:, hs], sg_ref[:, hs], sb_ref[:, hs]).astype(BF16)
        for c in range(rows // HEAD):
            rs = slice(c * HEAD, (c + 1) * HEAD)
            gate = _dot(sw_ref[h], vn[rs, :]) + sbias_ref[h]
            yb_ref[rs, hs] = (u[rs, hs] * gate).astype(BF16)


def _mix_in(x2d, ln_g, ln_b, w_in, cs, sgu_g, sgu_b, sgu_w, sgu_bias, d_a, d_b):
    rows, d = x2d.shape
    tm = min(ROW_TILE, rows)
    n_heads_b = d_b // HEAD
    row = lambda w: pl.BlockSpec((tm, w), lambda i: (i, 0))
    return pl.pallas_call(
        functools.partial(_mix_in_body, d_a=d_a, d_b=d_b),
        grid=(rows // tm,),
        in_specs=[row(d), _whole((1, d)), _whole((1, d)), _whole(w_in.shape),
                  _whole(cs.shape), _whole((1, d_b)), _whole((1, d_b)),
                  _whole((n_heads_b, HEAD, HEAD)), _whole((n_heads_b, HEAD, HEAD))],
        out_specs=[row(d_a), row(d_a), row(d_b)],
        out_shape=[jax.ShapeDtypeStruct((rows, d_a), BF16),
                   jax.ShapeDtypeStruct((rows, d_a), BF16),
                   jax.ShapeDtypeStruct((rows, d_b), BF16)],
        compiler_params=_params(("parallel",)),
        name="mix_in",
    )(x2d, ln_g, ln_b, w_in, cs, sgu_g, sgu_b, sgu_w, sgu_bias)


def _seq_dft_body(a_ref, q_ref, o_ref, acc_ref, *, seq, scale):
    i, j = pl.program_id(0), pl.program_id(1)
    n_batch, tk, _ = o_ref.shape
    ts = a_ref.shape[1]

    @pl.when(j == 0)
    def _():
        acc_ref[...] = jnp.zeros_like(acc_ref)

    k = i * tk + lax.broadcasted_iota(jnp.int32, (tk, ts), 0)
    s = j * ts + lax.broadcasted_iota(jnp.int32, (tk, ts), 1)
    angle = ((k * s) & (seq - 1)).astype(F32) * (2.0 * math.pi / seq)
    cos_t = jnp.cos(angle).astype(BF16)
    nsin_t = (-jnp.sin(angle)).astype(BF16)
    for b in range(n_batch):
        acc_ref[b] += _dot(cos_t, a_ref[b]) + _dot(nsin_t, q_ref[b])

    @pl.when(j == pl.num_programs(1) - 1)
    def _():
        o_ref[...] = (acc_ref[...] * scale).astype(BF16)


def _seq_dft(a3d, q3d):
    n_batch, seq, d_a = a3d.shape
    assert seq & (seq - 1) == 0, "sequence length must be a power of two"
    tk, ts = min(DFT_K_TILE, seq), min(DFT_S_TILE, seq)
    scale = 1.0 / math.sqrt(seq * HEAD)
    in_spec = pl.BlockSpec((n_batch, ts, d_a), lambda i, j: (0, j, 0))
    return pl.pallas_call(
        functools.partial(_seq_dft_body, seq=seq, scale=scale),
        grid=(seq // tk, seq // ts),
        in_specs=[in_spec, in_spec],
        out_specs=pl.BlockSpec((n_batch, tk, d_a), lambda i, j: (0, i, 0)),
        out_shape=jax.ShapeDtypeStruct((n_batch, seq, d_a), BF16),
        scratch_shapes=[pltpu.VMEM((n_batch, tk, d_a), F32)],
        compiler_params=_params(("parallel", "arbitrary")),
        name="seq_dft",
    )(a3d, q3d)


def _mix_out_body(x_ref, ya_ref, yb_ref, g0_ref, b0_ref, wo_ref, g1_ref, b1_ref,
                  rwh_ref, rwl_ref, rb_ref, x1_ref, x1b_ref, lg_ref, *, alpha, d_a):
    xln = _layer_norm(x_ref[...], g0_ref[...], b0_ref[...])
    mix = _dot(ya_ref[...], wo_ref[:d_a, :]) + _dot(yb_ref[...], wo_ref[d_a:, :])
    x1 = _layer_norm(alpha * xln + mix, g1_ref[...], b1_ref[...])
    x1_ref[...] = x1
    hi = x1.astype(BF16)
    x1b_ref[...] = hi
    lo = (x1 - hi.astype(F32)).astype(BF16)
    lg_ref[...] = (_dot(hi, rwh_ref[...]) + _dot(hi, rwl_ref[...])
                   + _dot(lo, rwh_ref[...]) + rb_ref[...])


def _mix_out(x2d, ya, yb, g0, b0, w_o, g1, b1, rw_hi, rw_lo, rb, alpha):
    rows, d = x2d.shape
    d_a = ya.shape[1]
    n_exp = rw_hi.shape[1]
    tm = min(ROW_TILE, rows)
    row = lambda w: pl.BlockSpec((tm, w), lambda i: (i, 0))
    vec = _whole((1, d))
    return pl.pallas_call(
        functools.partial(_mix_out_body, alpha=alpha, d_a=d_a),
        grid=(rows // tm,),
        in_specs=[row(d), row(d_a), row(yb.shape[1]), vec, vec, _whole(w_o.shape), vec, vec,
                  _whole(rw_hi.shape), _whole(rw_lo.shape), _whole((1, n_exp))],
        out_specs=[row(d), row(d), row(n_exp)],
        out_shape=[jax.ShapeDtypeStruct((rows, d), F32),
                   jax.ShapeDtypeStruct((rows, d), BF16),
                   jax.ShapeDtypeStruct((rows, n_exp), F32)],
        compiler_params=_params(("parallel",)),
        name="mix_out",
    )(x2d, ya, yb, g0, b0, w_o, g1, b1, rw_hi, rw_lo, rb)


def _moe_body(be_ref, nv_ref, x_ref, wg_ref, wl_ref, bg_ref, bl_ref, wd_ref, bd_ref, o_ref):
    del be_ref

    @pl.when(pl.program_id(0) < nv_ref[0])
    def _():
        x = x_ref[...]
        d_ff = wg_ref.shape[2]
        tf = min(MOE_FF_TILE, d_ff)
        for f in range(d_ff // tf):
            fs = slice(f * tf, (f + 1) * tf)
            glu = jnp.minimum(_dot(x, wg_ref[0, :, fs]) + bg_ref[0, :, fs], SWIGLU_LIMIT)
            lin = jnp.clip(_dot(x, wl_ref[0, :, fs]) + bl_ref[0, :, fs],
                           -SWIGLU_LIMIT, SWIGLU_LIMIT)
            act = glu * (1.0 / (1.0 + jnp.exp(-SWIGLU_ALPHA * glu))) * (lin + 1.0)
            part = _dot(act.astype(BF16), wd_ref[0, fs, :])
            if f == 0:
                o_ref[...] = part + bd_ref[0]
            else:
                o_ref[...] += part


def _moe(blk_e, n_valid, xin, w_g, w_l, b_g, b_l, w_d, b_d, bm):
    n_pad, d = xin.shape
    n_exp, _, d_ff = w_g.shape
    expert = lambda *tail: (lambda i, be, nv: (be[i],) + tail)
    single = dict(pipeline_mode=pl.Buffered(1))
    grid_spec = pltpu.PrefetchScalarGridSpec(
        num_scalar_prefetch=2,
        grid=(n_pad // bm,),
        in_specs=[
            pl.BlockSpec((bm, d), lambda i, be, nv: (jnp.minimum(i, nv[0] - 1), 0)),
            pl.BlockSpec((1, d, d_ff), expert(0, 0), **single),
            pl.BlockSpec((1, d, d_ff), expert(0, 0), **single),
            pl.BlockSpec((1, 1, d_ff), expert(0, 0)),
            pl.BlockSpec((1, 1, d_ff), expert(0, 0)),
            pl.BlockSpec((1, d_ff, d), expert(0, 0), **single),
            pl.BlockSpec((1, 1, d), expert(0, 0)),
        ],
        out_specs=pl.BlockSpec((bm, d), lambda i, be, nv: (i, 0)),
    )
    return pl.pallas_call(
        _moe_body,
        grid_spec=grid_spec,
        out_shape=jax.ShapeDtypeStruct((n_pad, d), F32),
        compiler_params=_params(("arbitrary",)),
        name="moe",
    )(blk_e, n_valid, xin, w_g, w_l, b_g, b_l, w_d, b_d)


def _ln_out_body(x1_ref, ff_ref, g_ref, b_ref, o_ref, *, alpha):
    o_ref[...] = _layer_norm(alpha * x1_ref[...] + ff_ref[...], g_ref[...], b_ref[...])


def _ln_out(x1, ff, g, b, alpha):
    rows, d = x1.shape
    tm = min(2 * ROW_TILE, rows)
    row = pl.BlockSpec((tm, d), lambda i: (i, 0))
    return pl.pallas_call(
        functools.partial(_ln_out_body, alpha=alpha),
        grid=(rows // tm,),
        in_specs=[row, row, _whole((1, d)), _whole((1, d))],
        out_specs=row,
        out_shape=jax.ShapeDtypeStruct((rows, d), F32),
        compiler_params=_params(("parallel",)),
        name="ln_out",
    )(x1, ff, g, b)


def _route(logits, n_exp, bm):
    t = logits.shape[0]
    tk = t * TOP_K
    top_vals, top_idx = lax.top_k(logits, TOP_K)
    gates = jax.nn.softmax(top_vals, axis=-1)
    flat_e = top_idx.reshape(-1)
    order = jnp.argsort(flat_e)
    se = flat_e[order]
    counts = jnp.bincount(flat_e, length=n_exp)
    starts = jnp.cumsum(counts) - counts
    padded = ((counts + bm - 1) // bm) * bm
    pends = jnp.cumsum(padded)
    pstarts = pends - padded
    dest_sorted = (pstarts[se] + jnp.arange(tk, dtype=jnp.int32) - starts[se]).astype(jnp.int32)
    n_pad = tk + n_exp * bm
    n_blocks = n_pad // bm
    src = jnp.zeros((n_pad,), jnp.int32).at[dest_sorted].set((order // TOP_K).astype(jnp.int32))
    dest = jnp.zeros((tk,), jnp.int32).at[order].set(dest_sorted).reshape(t, TOP_K)
    blk_e = jnp.minimum(jnp.searchsorted(pends, jnp.arange(n_blocks) * bm, side='right'),
                        n_exp - 1).astype(jnp.int32)
    n_valid = (pends[-1:] // bm).astype(jnp.int32)
    return gates, src, dest, blk_e, n_valid


def _channel_dft_table():
    n = np.arange(HEAD)
    ang = 2.0 * np.pi * ((n[:, None] * n[None, :]) % HEAD) / HEAD
    return jnp.asarray(np.concatenate([np.cos(ang), np.sin(ang)], axis=1), dtype=BF16)


def kernel(x_prompt, x_sample, ln_in_g, ln_in_b, w_in, sgu_norm_g, sgu_norm_b, sgu_w, sgu_b,
           w_o, ln1_g, ln1_b, router_w, router_b, w_gu, b_gu, w_down, b_down, ln2_g, ln2_b):
    depth, d, d_in = w_in.shape
    assert depth == 1, "single-layer encoder only"
    n_heads_b, chunk, _ = sgu_w.shape[1:]
    assert chunk == HEAD
    d_b = n_heads_b * HEAD
    d_a = d_in - 2 * d_b
    n_exp = router_w.shape[2]
    d_ff = w_down.shape[2]
    alpha = (2.0 * depth) ** 0.25
    vec = lambda p: p.reshape(1, -1)

    w_in_b = w_in[0].astype(BF16)
    w_o_b = w_o[0].astype(BF16)
    sgu_w_b = sgu_w[0].astype(BF16)
    sgu_bias = jnp.broadcast_to(sgu_b[0][:, :, None], (n_heads_b, HEAD, HEAD))
    cs = _channel_dft_table()
    rw_hi = router_w[0].astype(BF16)
    rw_lo = (router_w[0] - rw_hi.astype(F32)).astype(BF16)
    w_g = w_gu[0][:, :, 0::2].astype(BF16)
    w_l = w_gu[0][:, :, 1::2].astype(BF16)
    b_g = b_gu[0][:, None, 0::2]
    b_l = b_gu[0][:, None, 1::2]
    w_d = w_down[0].astype(BF16)
    b_d = b_down[0][:, None, :]

    x1_parts, x1b_parts, logit_parts = [], [], []
    for x in (x_prompt, x_sample):
        n_batch, seq, _ = x.shape
        x2d = x.reshape(n_batch * seq, d)
        a, q, yb = _mix_in(x2d, vec(ln_in_g), vec(ln_in_b), w_in_b, cs, vec(sgu_norm_g[0]),
                           vec(sgu_norm_b[0]), sgu_w_b, sgu_bias, d_a, d_b)
        ya = _seq_dft(a.reshape(n_batch, seq, d_a), q.reshape(n_batch, seq, d_a))
        x1, x1b, logits = _mix_out(x2d, ya.reshape(n_batch * seq, d_a), yb, vec(ln_in_g),
                                   vec(ln_in_b), w_o_b, vec(ln1_g[0]), vec(ln1_b[0]),
                                   rw_hi, rw_lo, vec(router_b[0]), alpha)
        x1_parts.append(x1)
        x1b_parts.append(x1b)
        logit_parts.append(logits)

    bm = min(MOE_ROWS, x1_parts[0].shape[0] * TOP_K)
    gates, src, dest, blk_e, n_valid = _route(jnp.concatenate(logit_parts, axis=0), n_exp, bm)
    xin = jnp.concatenate(x1b_parts, axis=0)[src]
    y = _moe(blk_e, n_valid, xin, w_g, w_l, b_g, b_l, w_d, b_d, bm)
    ff = jnp.sum(y[dest] * gates[:, :, None], axis=1)

    outs, row0 = [], 0
    for x, x1 in zip((x_prompt, x_sample), x1_parts):
        rows = x1.shape[0]
        out = _ln_out(x1, ff[row0:row0 + rows], vec(ln2_g[0]), vec(ln2_b[0]), alpha)
        outs.append(out.reshape(x.shape))
        row0 += rows
    return tuple(outs)
```

```python
import functools
import math

import jax
import jax.numpy as jnp
import numpy as np
from jax import lax
from jax.experimental import pallas as pl
from jax.experimental.pallas import tpu as pltpu

F32 = jnp.float32
BF16 = jnp.bfloat16

LN_EPS = 1e-5
TOP_K = 4
HEAD = 128
SWIGLU_LIMIT = 7.0
SWIGLU_ALPHA = 1.702
VMEM_LIMIT_BYTES = 56 * 1024 * 1024

ROW_TILE = 256
DFT_K_TILE = 512
DFT_S_TILE = 512
DFT_C_TILE = 256
MOE_ROWS = 512
MOE_FF_TILE = 512
PREP_ROWS = 512
SPLIT_GROUP = 512


def _layer_norm(x, g, b):
    mu = jnp.mean(x, axis=-1, keepdims=True)
    xc = x - mu
    var = jnp.mean(xc * xc, axis=-1, keepdims=True)
    return xc * lax.rsqrt(var + LN_EPS) * g + b


def _gelu(z):
    return 0.5 * z * (1.0 + lax.erf(z * (1.0 / math.sqrt(2.0))))


def _dot(a, b):
    return jnp.dot(a, b, preferred_element_type=F32)


def _params(semantics):
    return pltpu.CompilerParams(dimension_semantics=semantics,
                                vmem_limit_bytes=VMEM_LIMIT_BYTES)


def _whole(shape):
    return pl.BlockSpec(shape, lambda *_: (0,) * len(shape),
                        pipeline_mode=pl.Buffered(1))


def _mix_in_body(x_ref, g_ref, b_ref, win_ref, cs_ref, sg_ref, sb_ref, sw_ref, sbias_ref,
                 a_ref, q_ref, yb_ref, *, d_a, d_b):
    xb = _layer_norm(x_ref[...], g_ref[...], b_ref[...]).astype(BF16)
    rows = xb.shape[0]

    za = _dot(xb, win_ref[:, :d_a]).astype(BF16)
    for h in range(d_a // HEAD):
        hs = slice(h * HEAD, (h + 1) * HEAD)
        ab = _dot(za[:, hs], cs_ref[...])
        a_ref[:, hs] = ab[:, :HEAD].astype(BF16)
        q_ref[:, hs] = ab[:, HEAD:].astype(BF16)

    u = _gelu(_dot(xb, win_ref[:, d_a:d_a + d_b]))
    v = _gelu(_dot(xb, win_ref[:, d_a + d_b:]))
    for h in range(d_b // HEAD):
        hs = slice(h * HEAD, (h + 1) * HEAD)
        vn = _layer_norm(v[:, hs], sg_ref[:, hs], sb_ref[:, hs]).astype(BF16)
        for c in range(rows // HEAD):
            rs = slice(c * HEAD, (c + 1) * HEAD)
            gate = _dot(sw_ref[h], vn[rs, :]) + sbias_ref[h]
            yb_ref[rs, hs] = (u[rs, hs] * gate).astype(BF16)


def _mix_in(x2d, ln_g, ln_b, w_in, cs, sgu_g, sgu_b, sgu_w, sgu_bias, d_a, d_b):
    rows, d = x2d.shape
    tm = min(ROW_TILE, rows)
    n_heads_b = d_b // HEAD
    row = lambda w: pl.BlockSpec((tm, w), lambda i: (i, 0))
    return pl.pallas_call(
        functools.partial(_mix_in_body, d_a=d_a, d_b=d_b),
        grid=(rows // tm,),
        in_specs=[row(d), _whole((1, d)), _whole((1, d)), _whole(w_in.shape),
                  _whole(cs.shape), _whole((1, d_b)), _whole((1, d_b)),
                  _whole((n_heads_b, HEAD, HEAD)), _whole((n_heads_b, HEAD, HEAD))],
        out_specs=[row(d_a), row(d_a), row(d_b)],
        out_shape=[jax.ShapeDtypeStruct((rows, d_a), BF16),
                   jax.ShapeDtypeStruct((rows, d_a), BF16),
                   jax.ShapeDtypeStruct((rows, d_b), BF16)],
        compiler_params=_params(("parallel",)),
        name="mix_in",
    )(x2d, ln_g, ln_b, w_in, cs, sgu_g, sgu_b, sgu_w, sgu_bias)


def _seq_dft_body(a_ref, q_ref, c0_ref, s0_ref, perm_ref, lo_ref, hi_ref,
                  accp_ref, accz_ref, cb_ref, sb_ref, ca_ref, sa_ref, cd_ref, sd_ref,
                  *, seq, scale):
    i, j = pl.program_id(1), pl.program_id(2)
    n_batch, tk, _ = lo_ref.shape
    tkx, ts = cb_ref.shape
    w = 2.0 * math.pi / seq

    @pl.when(j == 0)
    def _():
        accp_ref[...] = jnp.zeros_like(accp_ref)
        accz_ref[...] = jnp.zeros_like(accz_ref)
        ds = lax.broadcasted_iota(jnp.int32, (1, ts), 1)
        ph = (((i * tk) * ds) & (seq - 1)).astype(F32) * w
        ci, si = jnp.cos(ph), jnp.sin(ph)
        cb_ref[...] = ci * c0_ref[...] - si * s0_ref[...]
        sb_ref[...] = si * c0_ref[...] + ci * s0_ref[...]
        k = i * tk + lax.broadcasted_iota(jnp.int32, (tkx, HEAD), 0)
        pd = ((k * ts) & (seq - 1)).astype(F32) * w
        cd_ref[...] = jnp.cos(pd)
        sd_ref[...] = jnp.sin(pd)
        ca_ref[...] = jnp.ones_like(ca_ref)
        sa_ref[...] = jnp.zeros_like(sa_ref)

    ca, sa = ca_ref[...], sa_ref[...]
    cos_parts, sin_parts = [], []
    for g in range(ts // HEAD):
        gs = slice(g * HEAD, (g + 1) * HEAD)
        cb, sb = cb_ref[:, gs], sb_ref[:, gs]
        cos_parts.append((ca * cb - sa * sb).astype(BF16))
        sin_parts.append((sa * cb + ca * sb).astype(BF16))
    cos_t = jnp.concatenate(cos_parts, axis=1)
    sin_t = jnp.concatenate(sin_parts, axis=1)
    for b in range(n_batch):
        accp_ref[b] += _dot(cos_t, a_ref[b])
        accz_ref[b] += _dot(sin_t, q_ref[b])
    cd, sd = cd_ref[...], sd_ref[...]
    ca_ref[...] = ca * cd - sa * sd
    sa_ref[...] = sa * cd + ca * sd

    @pl.when(j == pl.num_programs(2) - 1)
    def _():
        first_row = lax.broadcasted_iota(jnp.int32, (tk, 1), 0) == 0
        for b in range(n_batch):
            p, z = accp_ref[b], accz_ref[b]
            lo_ref[b] = ((p[:tk] - z[:tk]) * scale).astype(BF16)
            mirror = ((p + z) * scale).astype(BF16)
            rev = _dot(perm_ref[...], mirror[:tk]).astype(BF16)
            hi_ref[b] = jnp.where(first_row, mirror[tk:tk + 1], rev)


def _seq_dft(a3d, q3d):
    n_batch, seq, d_a = a3d.shape
    assert seq & (seq - 1) == 0, "sequence length must be a power of two"
    half = seq // 2
    tk, ts, tc = min(DFT_K_TILE, half), min(DFT_S_TILE, seq), min(DFT_C_TILE, d_a)
    tkx = tk + 8
    n_i = half // tk
    scale = 1.0 / math.sqrt(seq * HEAD)
    ang = 2.0 * np.pi * np.outer(np.arange(tkx), np.arange(ts)) / seq
    c0 = jnp.asarray(np.cos(ang), dtype=F32)
    s0 = jnp.asarray(np.sin(ang), dtype=F32)
    perm = np.zeros((tk, tk), np.float32)
    r = np.arange(1, tk)
    perm[r, tk - r] = 1.0
    in_spec = pl.BlockSpec((n_batch, ts, tc), lambda c, i, j: (0, j, c))
    out = jax.ShapeDtypeStruct((n_batch, half, d_a), BF16)
    return pl.pallas_call(
        functools.partial(_seq_dft_body, seq=seq, scale=scale),
        grid=(d_a // tc, n_i, seq // ts),
        in_specs=[in_spec, in_spec, _whole((tkx, ts)), _whole((tkx, ts)), _whole((tk, tk))],
        out_specs=[pl.BlockSpec((n_batch, tk, tc), lambda c, i, j: (0, i, c)),
                   pl.BlockSpec((n_batch, tk, tc), lambda c, i, j: (0, n_i - 1 - i, c))],
        out_shape=[out, out],
        scratch_shapes=[pltpu.VMEM((n_batch, tkx, tc), F32), pltpu.VMEM((n_batch, tkx, tc), F32),
                        pltpu.VMEM((tkx, ts), F32), pltpu.VMEM((tkx, ts), F32)]
                       + [pltpu.VMEM((tkx, HEAD), F32)] * 4,
        compiler_params=_params(("parallel", "parallel", "arbitrary")),
        name="seq_dft",
    )(a3d, q3d, c0, s0, jnp.asarray(perm, dtype=BF16))


def _mix_out_body(x_ref, lo_ref, hi_ref, yb_ref, g0_ref, b0_ref, wo_ref, g1_ref, b1_ref,
                  rwh_ref, rwl_ref, rb_ref, x1_ref, x1b_ref, lg_ref, *, alpha, d_a):
    xln = _layer_norm(x_ref[...], g0_ref[...], b0_ref[...])
    in_low_half = pl.program_id(1) < pl.num_programs(1) // 2
    ya = jnp.where(in_low_half, lo_ref[...], hi_ref[...])
    mix = _dot(ya, wo_ref[:d_a, :]) + _dot(yb_ref[...], wo_ref[d_a:, :])
    x1 = _layer_norm(alpha * xln + mix, g1_ref[...], b1_ref[...])
    x1_ref[...] = x1
    hi = x1.astype(BF16)
    x1b_ref[...] = hi
    lo = (x1 - hi.astype(F32)).astype(BF16)
    lg_ref[...] = (_dot(hi, rwh_ref[...]) + _dot(hi, rwl_ref[...])
                   + _dot(lo, rwh_ref[...]) + rb_ref[...])


def _mix_out(x2d, ya_lo, ya_hi, yb, g0, b0, w_o, g1, b1, rw_hi, rw_lo, rb, alpha):
    rows, d = x2d.shape
    n_batch, half, d_a = ya_lo.shape
    n_exp = rw_hi.shape[1]
    tm = min(ROW_TILE, half)
    n_p = 2 * half // tm
    row = lambda w: pl.BlockSpec((tm, w), lambda b, p: (b * n_p + p, 0))
    vec = _whole((1, d))
    return pl.pallas_call(
        functools.partial(_mix_out_body, alpha=alpha, d_a=d_a),
        grid=(n_batch, n_p),
        in_specs=[row(d),
                  pl.BlockSpec((None, tm, d_a), lambda b, p: (b, jnp.minimum(p, n_p // 2 - 1), 0)),
                  pl.BlockSpec((None, tm, d_a), lambda b, p: (b, jnp.maximum(p - n_p // 2, 0), 0)),
                  row(yb.shape[1]), vec, vec, _whole(w_o.shape), vec, vec,
                  _whole(rw_hi.shape), _whole(rw_lo.shape), _whole((1, n_exp))],
        out_specs=[row(d), row(d), row(n_exp)],
        out_shape=[jax.ShapeDtypeStruct((rows, d), F32),
                   jax.ShapeDtypeStruct((rows, d), BF16),
                   jax.ShapeDtypeStruct((rows, n_exp), F32)],
        compiler_params=_params(("parallel", "parallel")),
        name="mix_out",
    )(x2d, ya_lo, ya_hi, yb, g0, b0, w_o, g1, b1, rw_hi, rw_lo, rb)


def _split_gu_body(w_ref, perm_ref, g_ref, l_ref):
    half = SPLIT_GROUP // 2
    for c in range(w_ref.shape[1] // SPLIT_GROUP):
        w = w_ref[:, c * SPLIT_GROUP:(c + 1) * SPLIT_GROUP].astype(BF16)
        r = _dot(w, perm_ref[...])
        g_ref[:, c * half:(c + 1) * half] = r[:, :half].astype(BF16)
        l_ref[:, c * half:(c + 1) * half] = r[:, half:].astype(BF16)


def _split_gu(w2d):
    rows, cols = w2d.shape
    assert cols % SPLIT_GROUP == 0
    tr = min(PREP_ROWS, rows)
    perm = np.zeros((SPLIT_GROUP, SPLIT_GROUP), np.float32)
    j = np.arange(SPLIT_GROUP // 2)
    perm[2 * j, j] = 1.0
    perm[2 * j + 1, SPLIT_GROUP // 2 + j] = 1.0
    out = jax.ShapeDtypeStruct((rows, cols // 2), BF16)
    return pl.pallas_call(
        _split_gu_body,
        grid=(rows // tr,),
        in_specs=[pl.BlockSpec((tr, cols), lambda i: (i, 0)), _whole(perm.shape)],
        out_specs=[pl.BlockSpec((tr, cols // 2), lambda i: (i, 0))] * 2,
        out_shape=[out, out],
        compiler_params=_params(("parallel",)),
        name="split_gu",
    )(w2d, jnp.asarray(perm, dtype=BF16))


def _cast_body(w_ref, o_ref):
    o_ref[...] = w_ref[...].astype(BF16)


def _cast_bf16(w2d):
    rows, cols = w2d.shape
    tr = min(PREP_ROWS, rows)
    spec = pl.BlockSpec((tr, cols), lambda i: (i, 0))
    return pl.pallas_call(
        _cast_body,
        grid=(rows // tr,),
        in_specs=[spec],
        out_specs=spec,
        out_shape=jax.ShapeDtypeStruct((rows, cols), BF16),
        compiler_params=_params(("parallel",)),
        name="cast_bf16",
    )(w2d)


def _moe_body(be_ref, nv_ref, x_ref, wg_ref, wl_ref, bg_ref, bl_ref, wd_ref, bd_ref, o_ref):
    del be_ref

    @pl.when(pl.program_id(0) < nv_ref[0])
    def _():
        x = x_ref[...]
        d_ff = wg_ref.shape[2]
        tf = min(MOE_FF_TILE, d_ff)
        for f in range(d_ff // tf):
            fs = slice(f * tf, (f + 1) * tf)
            glu = jnp.minimum(_dot(x, wg_ref[0, :, fs]) + bg_ref[0, :, fs], SWIGLU_LIMIT)
            lin = jnp.clip(_dot(x, wl_ref[0, :, fs]) + bl_ref[0, :, fs],
                           -SWIGLU_LIMIT, SWIGLU_LIMIT)
            act = glu * (1.0 / (1.0 + jnp.exp(-SWIGLU_ALPHA * glu))) * (lin + 1.0)
            part = _dot(act.astype(BF16), wd_ref[0, fs, :])
            if f == 0:
                o_ref[...] = part + bd_ref[0]
            else:
                o_ref[...] += part


def _moe(blk_e, n_valid, xin, w_g, w_l, b_g, b_l, w_d, b_d, bm):
    n_pad, d = xin.shape
    n_exp, _, d_ff = w_g.shape
    expert = lambda *tail: (lambda i, be, nv: (be[i],) + tail)
    single = dict(pipeline_mode=pl.Buffered(1))
    grid_spec = pltpu.PrefetchScalarGridSpec(
        num_scalar_prefetch=2,
        grid=(n_pad // bm,),
        in_specs=[
            pl.BlockSpec((bm, d), lambda i, be, nv: (jnp.minimum(i, nv[0] - 1), 0)),
            pl.BlockSpec((1, d, d_ff), expert(0, 0), **single),
            pl.BlockSpec((1, d, d_ff), expert(0, 0), **single),
            pl.BlockSpec((1, 1, d_ff), expert(0, 0)),
            pl.BlockSpec((1, 1, d_ff), expert(0, 0)),
            pl.BlockSpec((1, d_ff, d), expert(0, 0), **single),
            pl.BlockSpec((1, 1, d), expert(0, 0)),
        ],
        out_specs=pl.BlockSpec((bm, d), lambda i, be, nv: (i, 0)),
    )
    return pl.pallas_call(
        _moe_body,
        grid_spec=grid_spec,
        out_shape=jax.ShapeDtypeStruct((n_pad, d), F32),
        compiler_params=_params(("arbitrary",)),
        name="moe",
    )(blk_e, n_valid, xin, w_g, w_l, b_g, b_l, w_d, b_d)


def _ln_out_body(x1_ref, ff_ref, g_ref, b_ref, o_ref, *, alpha):
    o_ref[...] = _layer_norm(alpha * x1_ref[...] + ff_ref[...], g_ref[...], b_ref[...])


def _ln_out(x1, ff, g, b, alpha):
    rows, d = x1.shape
    tm = min(2 * ROW_TILE, rows)
    row = pl.BlockSpec((tm, d), lambda i: (i, 0))
    return pl.pallas_call(
        functools.partial(_ln_out_body, alpha=alpha),
        grid=(rows // tm,),
        in_specs=[row, row, _whole((1, d)), _whole((1, d))],
        out_specs=row,
        out_shape=jax.ShapeDtypeStruct((rows, d), F32),
        compiler_params=_params(("parallel",)),
        name="ln_out",
    )(x1, ff, g, b)


def _route(logits, n_exp, bm):
    t = logits.shape[0]
    tk = t * TOP_K
    top_vals, top_idx = lax.top_k(logits, TOP_K)
    gates = jax.nn.softmax(top_vals, axis=-1)
    flat_e = top_idx.reshape(-1).astype(jnp.int32)
    onehot = (flat_e[:, None] == jnp.arange(n_exp, dtype=jnp.int32)[None, :]).astype(jnp.int32)
    csum = jnp.cumsum(onehot, axis=0)
    counts = csum[-1]
    starts = jnp.cumsum(counts) - counts
    padded = ((counts + bm - 1) // bm) * bm
    pends = jnp.cumsum(padded)
    pstarts = pends - padded
    dest = jnp.sum(onehot * (csum - 1 + pstarts[None, :]), axis=1).reshape(t, TOP_K)
    n_blocks = (tk + n_exp * bm) // bm
    blk_start = jnp.arange(n_blocks, dtype=jnp.int32) * bm
    blk_e = jnp.minimum(jnp.searchsorted(pends, blk_start, side='right'),
                        n_exp - 1).astype(jnp.int32)
    n_valid = (pends[-1:] // bm).astype(jnp.int32)
    tok_sorted = jnp.concatenate([(jnp.argsort(flat_e) // TOP_K).astype(jnp.int32),
                                  jnp.zeros((bm,), jnp.int32)])
    row_in_expert = blk_start - pstarts[blk_e]
    run0 = jnp.clip(starts[blk_e] + row_in_expert, 0, tk)
    src = jax.vmap(lambda s: lax.dynamic_slice(tok_sorted, (s,), (bm,)))(run0)
    valid = (row_in_expert[:, None] + jnp.arange(bm, dtype=jnp.int32)[None, :]
             < counts[blk_e][:, None])
    src = jnp.where(valid, src, 0).reshape(-1)
    return gates, src, dest, blk_e, n_valid


def _channel_dft_table():
    n = np.arange(HEAD)
    ang = 2.0 * np.pi * ((n[:, None] * n[None, :]) % HEAD) / HEAD
    return jnp.asarray(np.concatenate([np.cos(ang), np.sin(ang)], axis=1), dtype=BF16)


def kernel(x_prompt, x_sample, ln_in_g, ln_in_b, w_in, sgu_norm_g, sgu_norm_b, sgu_w, sgu_b,
           w_o, ln1_g, ln1_b, router_w, router_b, w_gu, b_gu, w_down, b_down, ln2_g, ln2_b):
    depth, d, d_in = w_in.shape
    assert depth == 1, "single-layer encoder only"
    n_heads_b, chunk, _ = sgu_w.shape[1:]
    assert chunk == HEAD
    d_b = n_heads_b * HEAD
    d_a = d_in - 2 * d_b
    n_exp = router_w.shape[2]
    d_ff = w_down.shape[2]
    alpha = (2.0 * depth) ** 0.25
    vec = lambda p: p.reshape(1, -1)

    w_in_b = w_in[0].astype(BF16)
    w_o_b = w_o[0].astype(BF16)
    sgu_w_b = sgu_w[0].astype(BF16)
    sgu_bias = jnp.broadcast_to(sgu_b[0][:, :, None], (n_heads_b, HEAD, HEAD))
    cs = _channel_dft_table()
    rw_hi = router_w[0].astype(BF16)
    rw_lo = (router_w[0] - rw_hi.astype(F32)).astype(BF16)
    w_g, w_l = _split_gu(w_gu[0].reshape(n_exp * d, 2 * d_ff))
    w_g = w_g.reshape(n_exp, d, d_ff)
    w_l = w_l.reshape(n_exp, d, d_ff)
    b_g = b_gu[0][:, None, 0::2]
    b_l = b_gu[0][:, None, 1::2]
    w_d = _cast_bf16(w_down[0].reshape(n_exp * d_ff, d)).reshape(n_exp, d_ff, d)
    b_d = b_down[0][:, None, :]

    x1_parts, x1b_parts, logit_parts = [], [], []
    for x in (x_prompt, x_sample):
        n_batch, seq, _ = x.shape
        x2d = x.reshape(n_batch * seq, d)
        a, q, yb = _mix_in(x2d, vec(ln_in_g), vec(ln_in_b), w_in_b, cs, vec(sgu_norm_g[0]),
                           vec(sgu_norm_b[0]), sgu_w_b, sgu_bias, d_a, d_b)
        ya_lo, ya_hi = _seq_dft(a.reshape(n_batch, seq, d_a), q.reshape(n_batch, seq, d_a))
        x1, x1b, logits = _mix_out(x2d, ya_lo, ya_hi, yb, vec(ln_in_g),
                                   vec(ln_in_b), w_o_b, vec(ln1_g[0]), vec(ln1_b[0]),
                                   rw_hi, rw_lo, vec(router_b[0]), alpha)
        x1_parts.append(x1)
        x1b_parts.append(x1b)
        logit_parts.append(logits)

    bm = min(MOE_ROWS, x1_parts[0].shape[0] * TOP_K)
    gates, src, dest, blk_e, n_valid = _route(jnp.concatenate(logit_parts, axis=0), n_exp, bm)
    xin = jnp.concatenate(x1b_parts, axis=0)[src]
    y = _moe(blk_e, n_valid, xin, w_g, w_l, b_g, b_l, w_d, b_d, bm)
    ff = sum(y[dest[:, c]] * gates[:, c:c + 1] for c in range(TOP_K))

    outs, row0 = [], 0
    for x, x1 in zip((x_prompt, x_sample), x1_parts):
        rows = x1.shape[0]
        out = _ln_out(x1, ff[row0:row0 + rows], vec(ln2_g[0]), vec(ln2_b[0]), alpha)
        outs.append(out.reshape(x.shape))
        row0 += rows
    return tuple(outs)
```

```python
import functools
import math

import jax
import jax.numpy as jnp
import numpy as np
from jax import lax
from jax.experimental import pallas as pl
from jax.experimental.pallas import tpu as pltpu

F32 = jnp.float32
BF16 = jnp.bfloat16

LN_EPS = 1e-5
TOP_K = 4
HEAD = 128
LANES = 128
SWIGLU_LIMIT = 7.0
SWIGLU_ALPHA = 1.702
VMEM_LIMIT_BYTES = 56 * 1024 * 1024

ROW_TILE = 256
DFT_K_TILE = 512
DFT_S_TILE = 512
DFT_C_TILE = 256
MOE_ROWS = 512
MOE_FF_TILE = 512
MOE_OUT_TILE = 512
PREP_ROWS = 512
SPLIT_GROUP = 512
PREP_STREAMS = 4


def _layer_norm(x, g, b):
    mu = jnp.mean(x, axis=-1, keepdims=True)
    xc = x - mu
    var = jnp.mean(xc * xc, axis=-1, keepdims=True)
    return xc * lax.rsqrt(var + LN_EPS) * g + b


def _gelu(z):
    return 0.5 * z * (1.0 + lax.erf(z * (1.0 / math.sqrt(2.0))))


def _dot(a, b):
    return jnp.dot(a, b, preferred_element_type=F32)


def _params(semantics):
    return pltpu.CompilerParams(dimension_semantics=semantics,
                                vmem_limit_bytes=VMEM_LIMIT_BYTES)


def _whole(shape):
    return pl.BlockSpec(shape, lambda *_: (0,) * len(shape),
                        pipeline_mode=pl.Buffered(1))


def _store_token_rows(ref, val):
    rpt = val.shape[1] // LANES
    for c in range(rpt):
        ref[pl.ds(c, val.shape[0], stride=rpt), :] = val[:, c * LANES:(c + 1) * LANES]


def _load_token_rows(ref, n_tokens, rpt):
    return [ref[pl.ds(c, n_tokens, stride=rpt), :] for c in range(rpt)]


def _mix_in_body(x_ref, g_ref, b_ref, win_ref, cs_ref, sg_ref, sb_ref, sw_ref, sbias_ref,
                 a_ref, q_ref, yb_ref, *, d_a, d_b):
    xb = _layer_norm(x_ref[...], g_ref[...], b_ref[...]).astype(BF16)
    rows = xb.shape[0]

    za = _dot(xb, win_ref[:, :d_a]).astype(BF16)
    for h in range(d_a // HEAD):
        hs = slice(h * HEAD, (h + 1) * HEAD)
        ab = _dot(za[:, hs], cs_ref[...])
        a_ref[:, hs] = ab[:, :HEAD].astype(BF16)
        q_ref[:, hs] = ab[:, HEAD:].astype(BF16)

    u = _gelu(_dot(xb, win_ref[:, d_a:d_a + d_b]))
    v = _gelu(_dot(xb, win_ref[:, d_a + d_b:]))
    for h in range(d_b // HEAD):
        hs = slice(h * HEAD, (h + 1) * HEAD)
        vn = _layer_norm(v[:, hs], sg_ref[:, hs], sb_ref[:, hs]).astype(BF16)
        for c in range(rows // HEAD):
            rs = slice(c * HEAD, (c + 1) * HEAD)
            gate = _dot(sw_ref[h], vn[rs, :]) + sbias_ref[h]
            yb_ref[rs, hs] = (u[rs, hs] * gate).astype(BF16)


def _mix_in(x2d, ln_g, ln_b, w_in, cs, sgu_g, sgu_b, sgu_w, sgu_bias, d_a, d_b):
    rows, d = x2d.shape
    tm = min(ROW_TILE, rows)
    n_heads_b = d_b // HEAD
    row = lambda w: pl.BlockSpec((tm, w), lambda i: (i, 0))
    return pl.pallas_call(
        functools.partial(_mix_in_body, d_a=d_a, d_b=d_b),
        grid=(rows // tm,),
        in_specs=[row(d), _whole((1, d)), _whole((1, d)), _whole(w_in.shape),
                  _whole(cs.shape), _whole((1, d_b)), _whole((1, d_b)),
                  _whole((n_heads_b, HEAD, HEAD)), _whole((n_heads_b, HEAD, HEAD))],
        out_specs=[row(d_a), row(d_a), row(d_b)],
        out_shape=[jax.ShapeDtypeStruct((rows, d_a), BF16),
                   jax.ShapeDtypeStruct((rows, d_a), BF16),
                   jax.ShapeDtypeStruct((rows, d_b), BF16)],
        compiler_params=_params(("parallel",)),
        name="mix_in",
    )(x2d, ln_g, ln_b, w_in, cs, sgu_g, sgu_b, sgu_w, sgu_bias)


def _seq_dft_body(a_ref, q_ref, c0_ref, s0_ref, perm_ref, lo_ref, hi_ref,
                  accp_ref, accz_ref, cb_ref, sb_ref, ca_ref, sa_ref, cd_ref, sd_ref,
                  *, seq, scale):
    i, j = pl.program_id(1), pl.program_id(2)
    n_batch, tk, _ = lo_ref.shape
    tkx, ts = cb_ref.shape
    w = 2.0 * math.pi / seq

    @pl.when(j == 0)
    def _():
        accp_ref[...] = jnp.zeros_like(accp_ref)
        accz_ref[...] = jnp.zeros_like(accz_ref)
        ds = lax.broadcasted_iota(jnp.int32, (1, ts), 1)
        ph = (((i * tk) * ds) & (seq - 1)).astype(F32) * w
        ci, si = jnp.cos(ph), jnp.sin(ph)
        cb_ref[...] = ci * c0_ref[...] - si * s0_ref[...]
        sb_ref[...] = si * c0_ref[...] + ci * s0_ref[...]
        k = i * tk + lax.broadcasted_iota(jnp.int32, (tkx, HEAD), 0)
        pd = ((k * ts) & (seq - 1)).astype(F32) * w
        cd_ref[...] = jnp.cos(pd)
        sd_ref[...] = jnp.sin(pd)
        ca_ref[...] = jnp.ones_like(ca_ref)
        sa_ref[...] = jnp.zeros_like(sa_ref)

    ca, sa = ca_ref[...], sa_ref[...]
    cos_parts, sin_parts = [], []
    for g in range(ts // HEAD):
        gs = slice(g * HEAD, (g + 1) * HEAD)
        cb, sb = cb_ref[:, gs], sb_ref[:, gs]
        cos_parts.append((ca * cb - sa * sb).astype(BF16))
        sin_parts.append((sa * cb + ca * sb).astype(BF16))
    cos_t = jnp.concatenate(cos_parts, axis=1)
    sin_t = jnp.concatenate(sin_parts, axis=1)
    for b in range(n_batch):
        accp_ref[b] += _dot(cos_t, a_ref[b])
        accz_ref[b] += _dot(sin_t, q_ref[b])
    cd, sd = cd_ref[...], sd_ref[...]
    ca_ref[...] = ca * cd - sa * sd
    sa_ref[...] = sa * cd + ca * sd

    @pl.when(j == pl.num_programs(2) - 1)
    def _():
        first_row = lax.broadcasted_iota(jnp.int32, (tk, 1), 0) == 0
        for b in range(n_batch):
            p, z = accp_ref[b], accz_ref[b]
            lo_ref[b] = ((p[:tk] - z[:tk]) * scale).astype(BF16)
            mirror = ((p + z) * scale).astype(BF16)
            rev = _dot(perm_ref[...], mirror[:tk]).astype(BF16)
            hi_ref[b] = jnp.where(first_row, mirror[tk:tk + 1], rev)


def _seq_dft(a3d, q3d):
    n_batch, seq, d_a = a3d.shape
    assert seq & (seq - 1) == 0, "sequence length must be a power of two"
    half = seq // 2
    tk, ts, tc = min(DFT_K_TILE, half), min(DFT_S_TILE, seq), min(DFT_C_TILE, d_a)
    tkx = tk + 8
    n_i = half // tk
    scale = 1.0 / math.sqrt(seq * HEAD)
    ang = 2.0 * np.pi * np.outer(np.arange(tkx), np.arange(ts)) / seq
    c0 = jnp.asarray(np.cos(ang), dtype=F32)
    s0 = jnp.asarray(np.sin(ang), dtype=F32)
    perm = np.zeros((tk, tk), np.float32)
    r = np.arange(1, tk)
    perm[r, tk - r] = 1.0
    in_spec = pl.BlockSpec((n_batch, ts, tc), lambda c, i, j: (0, j, c))
    out = jax.ShapeDtypeStruct((n_batch, half, d_a), BF16)
    return pl.pallas_call(
        functools.partial(_seq_dft_body, seq=seq, scale=scale),
        grid=(d_a // tc, n_i, seq // ts),
        in_specs=[in_spec, in_spec, _whole((tkx, ts)), _whole((tkx, ts)), _whole((tk, tk))],
        out_specs=[pl.BlockSpec((n_batch, tk, tc), lambda c, i, j: (0, i, c)),
                   pl.BlockSpec((n_batch, tk, tc), lambda c, i, j: (0, n_i - 1 - i, c))],
        out_shape=[out, out],
        scratch_shapes=[pltpu.VMEM((n_batch, tkx, tc), F32), pltpu.VMEM((n_batch, tkx, tc), F32),
                        pltpu.VMEM((tkx, ts), F32), pltpu.VMEM((tkx, ts), F32)]
                       + [pltpu.VMEM((tkx, HEAD), F32)] * 4,
        compiler_params=_params(("parallel", "parallel", "arbitrary")),
        name="seq_dft",
    )(a3d, q3d, c0, s0, jnp.asarray(perm, dtype=BF16))


def _mix_out_body(x_ref, lo_ref, hi_ref, yb_ref, g0_ref, b0_ref, wo_ref, g1_ref, b1_ref,
                  rwh_ref, rwl_ref, rb_ref, x1r_in_ref, x1r_ref, lg_ref, *, alpha, d_a):
    del x1r_in_ref
    xln = _layer_norm(x_ref[...], g0_ref[...], b0_ref[...])
    in_low_half = pl.program_id(1) < pl.num_programs(1) // 2
    ya = jnp.where(in_low_half, lo_ref[...], hi_ref[...])
    mix = _dot(ya, wo_ref[:d_a, :]) + _dot(yb_ref[...], wo_ref[d_a:, :])
    x1 = _layer_norm(alpha * xln + mix, g1_ref[...], b1_ref[...])
    _store_token_rows(x1r_ref, x1)
    hi = x1.astype(BF16)
    lo = (x1 - hi.astype(F32)).astype(BF16)
    lg_ref[...] = (_dot(hi, rwh_ref[...]) + _dot(hi, rwl_ref[...])
                   + _dot(lo, rwh_ref[...]) + rb_ref[...])


def _mix_out(x2d, ya_lo, ya_hi, yb, g0, b0, w_o, g1, b1, rw_hi, rw_lo, rb, alpha, row0, x1r_all):
    rows, d = x2d.shape
    n_batch, half, d_a = ya_lo.shape
    n_exp = rw_hi.shape[1]
    tm = min(ROW_TILE, half)
    n_p = 2 * half // tm
    rpt = d // LANES
    assert row0 % tm == 0
    row = lambda w: pl.BlockSpec((tm, w), lambda b, p: (b * n_p + p, 0))
    vec = _whole((1, d))
    return pl.pallas_call(
        functools.partial(_mix_out_body, alpha=alpha, d_a=d_a),
        grid=(n_batch, n_p),
        in_specs=[row(d),
                  pl.BlockSpec((None, tm, d_a), lambda b, p: (b, jnp.minimum(p, n_p // 2 - 1), 0)),
                  pl.BlockSpec((None, tm, d_a), lambda b, p: (b, jnp.maximum(p - n_p // 2, 0), 0)),
                  row(yb.shape[1]), vec, vec, _whole(w_o.shape), vec, vec,
                  _whole(rw_hi.shape), _whole(rw_lo.shape), _whole((1, n_exp)),
                  pl.BlockSpec(memory_space=pl.ANY)],
        out_specs=[pl.BlockSpec((tm * rpt, LANES), lambda b, p: (row0 // tm + b * n_p + p, 0)),
                   row(n_exp)],
        out_shape=[jax.ShapeDtypeStruct(x1r_all.shape, F32),
                   jax.ShapeDtypeStruct((rows, n_exp), F32)],
        input_output_aliases={12: 0},
        compiler_params=_params(("parallel", "parallel")),
        name="mix_out",
    )(x2d, ya_lo, ya_hi, yb, g0, b0, w_o, g1, b1, rw_hi, rw_lo, rb, x1r_all)


def _column_streams(w2d, tr):
    rows, cols = w2d.shape
    n = PREP_STREAMS if cols % (PREP_STREAMS * SPLIT_GROUP) == 0 else 1
    specs = [pl.BlockSpec((tr, cols // n), functools.partial(lambda i, q: (i, q), q=q))
             for q in range(n)]
    return [w2d] * n, specs


def _split_gu_body(*refs):
    *w_refs, perm_ref, g_ref, l_ref = refs
    half = SPLIT_GROUP // 2
    groups = w_refs[0].shape[1] // SPLIT_GROUP
    for q, w_ref in enumerate(w_refs):
        for c in range(groups):
            w = w_ref[:, c * SPLIT_GROUP:(c + 1) * SPLIT_GROUP].astype(BF16)
            r = _dot(w, perm_ref[...])
            o = (q * groups + c) * half
            g_ref[:, o:o + half] = r[:, :half].astype(BF16)
            l_ref[:, o:o + half] = r[:, half:].astype(BF16)


def _split_gu(w2d):
    rows, cols = w2d.shape
    assert cols % SPLIT_GROUP == 0
    tr = min(PREP_ROWS, rows)
    perm = np.zeros((SPLIT_GROUP, SPLIT_GROUP), np.float32)
    j = np.arange(SPLIT_GROUP // 2)
    perm[2 * j, j] = 1.0
    perm[2 * j + 1, SPLIT_GROUP // 2 + j] = 1.0
    streams, specs = _column_streams(w2d, tr)
    out = jax.ShapeDtypeStruct((rows, cols // 2), BF16)
    return pl.pallas_call(
        _split_gu_body,
        grid=(rows // tr,),
        in_specs=specs + [_whole(perm.shape)],
        out_specs=[pl.BlockSpec((tr, cols // 2), lambda i: (i, 0))] * 2,
        out_shape=[out, out],
        compiler_params=_params(("parallel",)),
        name="split_gu",
    )(*streams, jnp.asarray(perm, dtype=BF16))


def _cast_body(*refs):
    *w_refs, o_ref = refs
    width = w_refs[0].shape[1]
    for q, w_ref in enumerate(w_refs):
        o_ref[:, q * width:(q + 1) * width] = w_ref[...].astype(BF16)


def _cast_bf16(w2d):
    rows, cols = w2d.shape
    tr = min(PREP_ROWS, rows)
    streams, specs = _column_streams(w2d, tr)
    return pl.pallas_call(
        _cast_body,
        grid=(rows // tr,),
        in_specs=specs,
        out_specs=pl.BlockSpec((tr, cols), lambda i: (i, 0)),
        out_shape=jax.ShapeDtypeStruct((rows, cols), BF16),
        compiler_params=_params(("parallel",)),
        name="cast_bf16",
    )(*streams)


def _moe_body(be_ref, nv_ref, src_ref, src_next_ref, dst_ref, dst_prev_ref, x_hbm, wg_ref,
              wl_ref, bg_ref, bl_ref, wd_ref, bd_ref, y_hbm, xbuf_ref, xs_ref, act_ref, obuf_ref,
              gsem, ssem):
    del be_ref
    i, n_valid = pl.program_id(0), nv_ref[0]
    bm, d = xs_ref.shape
    rpt = d // LANES
    slot = i % 2

    def gather_copy(idx_ref, r, to_slot):
        tok = idx_ref[0, 0, r]
        return pltpu.make_async_copy(x_hbm.at[pl.ds(pl.multiple_of(tok * rpt, rpt), rpt), :],
                                     xbuf_ref.at[to_slot, pl.ds(r * rpt, rpt), :], gsem)

    def scatter_copy(idx_ref, r):
        row = idx_ref[0, 0, r]
        return pltpu.make_async_copy(obuf_ref.at[pl.ds(r * rpt, rpt), :],
                                     y_hbm.at[pl.ds(pl.multiple_of(row * rpt, rpt), rpt), :], ssem)

    def wait_gather(of_slot):
        pltpu.make_async_copy(x_hbm.at[pl.ds(0, bm * rpt), :], xbuf_ref.at[of_slot], gsem).wait()

    def wait_scatter():
        pltpu.make_async_copy(obuf_ref, y_hbm.at[pl.ds(0, bm * rpt), :], ssem).wait()

    @pl.when(i < n_valid)
    def _():
        @pl.when(i == 0)
        def _():
            obuf_ref[...] = jnp.zeros_like(obuf_ref)
            for r in range(bm):
                gather_copy(src_ref, r, 0).start()

        wait_gather(slot)
        for c, part in enumerate(_load_token_rows(xbuf_ref.at[slot], bm, rpt)):
            xs_ref[:, c * LANES:(c + 1) * LANES] = part.astype(BF16)
        for r in range(bm):
            gather_copy(src_next_ref, r, 1 - slot).start()
        for r in range(bm):
            scatter_copy(dst_prev_ref, r).start()

        x = xs_ref[...]
        d_ff = wg_ref.shape[2]
        tf = min(MOE_FF_TILE, d_ff)
        for f in range(d_ff // tf):
            fs = slice(f * tf, (f + 1) * tf)
            glu = jnp.minimum(_dot(x, wg_ref[0, :, fs]) + bg_ref[0, :, fs], SWIGLU_LIMIT)
            lin = jnp.clip(_dot(x, wl_ref[0, :, fs]) + bl_ref[0, :, fs],
                           -SWIGLU_LIMIT, SWIGLU_LIMIT)
            act = glu * (1.0 / (1.0 + jnp.exp(-SWIGLU_ALPHA * glu))) * (lin + 1.0)
            act_ref[:, fs] = act.astype(BF16)

        wait_scatter()
        tn = min(MOE_OUT_TILE, d)
        for n in range(d // tn):
            ns = slice(n * tn, (n + 1) * tn)
            done = _dot(act_ref[...], wd_ref[0, :, ns]) + bd_ref[0, :, ns]
            for c in range(tn // LANES):
                obuf_ref[pl.ds(n * (tn // LANES) + c, bm, stride=rpt), :] = (
                    done[:, c * LANES:(c + 1) * LANES])

        @pl.when(i == n_valid - 1)
        def _():
            for r in range(bm):
                scatter_copy(dst_ref, r).start()
            wait_gather(1 - slot)
            wait_scatter()


def _moe(blk_e, n_valid, src, dst, x1r, w_g, w_l, b_g, b_l, w_d, b_d, n_out_rows):
    n_blocks, _, bm = src.shape
    n_exp, d, d_ff = w_g.shape
    rpt = d // LANES
    expert = lambda i, be, nv: (be[i], 0, 0)
    single = dict(pipeline_mode=pl.Buffered(1))
    idx_block = lambda index_map: pl.BlockSpec((1, 1, bm), index_map, memory_space=pltpu.SMEM)
    grid_spec = pltpu.PrefetchScalarGridSpec(
        num_scalar_prefetch=2,
        grid=(n_blocks,),
        in_specs=[
            idx_block(lambda i, be, nv: (i, 0, 0)),
            idx_block(lambda i, be, nv: (jnp.minimum(i + 1, n_blocks - 1), 0, 0)),
            idx_block(lambda i, be, nv: (i, 0, 0)),
            idx_block(lambda i, be, nv: (jnp.maximum(i - 1, 0), 0, 0)),
            pl.BlockSpec(memory_space=pl.ANY),
            pl.BlockSpec((1, d, d_ff), expert, **single),
            pl.BlockSpec((1, d, d_ff), expert, **single),
            pl.BlockSpec((1, 1, d_ff), expert),
            pl.BlockSpec((1, 1, d_ff), expert),
            pl.BlockSpec((1, d_ff, d), expert, **single),
            pl.BlockSpec((1, 1, d), expert),
        ],
        out_specs=pl.BlockSpec(memory_space=pl.ANY),
        scratch_shapes=[pltpu.VMEM((2, bm * rpt, LANES), F32), pltpu.VMEM((bm, d), BF16),
                        pltpu.VMEM((bm, d_ff), BF16), pltpu.VMEM((bm * rpt, LANES), F32),
                        pltpu.SemaphoreType.DMA(()), pltpu.SemaphoreType.DMA(())],
    )
    return pl.pallas_call(
        _moe_body,
        grid_spec=grid_spec,
        out_shape=jax.ShapeDtypeStruct((n_out_rows * rpt, LANES), F32),
        compiler_params=_params(("arbitrary",)),
        name="moe",
    )(blk_e, n_valid, src, src, dst, dst, x1r, w_g, w_l, b_g, b_l, w_d, b_d)


def _ln_out_body(x1r_ref, gate_ref, *refs, alpha):
    *y_refs, g_ref, b_ref, o_ref = refs
    tm, d = o_ref.shape
    rpt = d // LANES
    gates = gate_ref[...]
    parts = [alpha * p for p in _load_token_rows(x1r_ref, tm, rpt)]
    for c, y_ref in enumerate(y_refs):
        gate = jnp.broadcast_to(gates[:, c:c + 1], (tm, LANES))
        parts = [p + gate * y for p, y in zip(parts, _load_token_rows(y_ref, tm, rpt))]
    o_ref[...] = _layer_norm(jnp.concatenate(parts, axis=1), g_ref[...], b_ref[...])


def _ln_out(x1r, y_rows, gates, g, b, alpha, rows, row0, total_rows):
    d = g.shape[1]
    rpt = d // LANES
    tm = min(ROW_TILE, rows)
    assert row0 % tm == 0 and total_rows % tm == 0
    tile0 = row0 // tm
    tok = lambda off: pl.BlockSpec((tm * rpt, LANES), lambda i: (off + tile0 + i, 0))
    return pl.pallas_call(
        functools.partial(_ln_out_body, alpha=alpha),
        grid=(rows // tm,),
        in_specs=[tok(0), pl.BlockSpec((tm, TOP_K), lambda i: (tile0 + i, 0))]
                 + [tok(c * (total_rows // tm)) for c in range(TOP_K)]
                 + [_whole((1, d)), _whole((1, d))],
        out_specs=pl.BlockSpec((tm, d), lambda i: (i, 0)),
        out_shape=jax.ShapeDtypeStruct((rows, d), F32),
        compiler_params=_params(("parallel",)),
        name="ln_out",
    )(x1r, gates, *([y_rows] * TOP_K), g, b)


def _route(logits, n_exp, bm):
    t = logits.shape[0]
    tk = t * TOP_K
    top_vals, top_idx = lax.top_k(logits, TOP_K)
    gates = jax.nn.softmax(top_vals, axis=-1)
    flat_e = top_idx.reshape(-1).astype(jnp.int32)
    sorted_e, order = lax.sort_key_val(flat_e, jnp.arange(tk, dtype=jnp.int32))
    experts = jnp.arange(n_exp, dtype=jnp.int32)
    starts = jnp.searchsorted(sorted_e, experts, side='left').astype(jnp.int32)
    counts = jnp.searchsorted(sorted_e, experts, side='right').astype(jnp.int32) - starts
    padded = ((counts + bm - 1) // bm) * bm
    pends = jnp.cumsum(padded)
    pstarts = pends - padded
    n_blocks = (tk + n_exp * bm) // bm
    blk_start = jnp.arange(n_blocks, dtype=jnp.int32) * bm
    blk_e = jnp.minimum(jnp.searchsorted(pends, blk_start, side='right'),
                        n_exp - 1).astype(jnp.int32)
    n_valid = (pends[-1:] // bm).astype(jnp.int32)
    lane = jnp.arange(bm, dtype=jnp.int32)[None, :]
    row_in_expert = (blk_start - pstarts[blk_e])[:, None] + lane
    valid = row_in_expert < counts[blk_e][:, None]
    pick = order[jnp.clip(starts[blk_e][:, None] + row_in_expert, 0, tk - 1)]
    tok, choice = pick // TOP_K, pick % TOP_K
    src = jnp.where(valid, tok, 0)
    spare = tk + blk_start[:, None] + lane - (starts + counts)[blk_e][:, None]
    dst = jnp.where(valid, choice * t + tok, spare)
    return gates, src[:, None, :], dst[:, None, :], blk_e, n_valid


def _channel_dft_table():
    n = np.arange(HEAD)
    ang = 2.0 * np.pi * ((n[:, None] * n[None, :]) % HEAD) / HEAD
    return jnp.asarray(np.concatenate([np.cos(ang), np.sin(ang)], axis=1), dtype=BF16)


def kernel(x_prompt, x_sample, ln_in_g, ln_in_b, w_in, sgu_norm_g, sgu_norm_b, sgu_w, sgu_b,
           w_o, ln1_g, ln1_b, router_w, router_b, w_gu, b_gu, w_down, b_down, ln2_g, ln2_b):
    depth, d, d_in = w_in.shape
    assert depth == 1, "single-layer encoder only"
    n_heads_b, chunk, _ = sgu_w.shape[1:]
    assert chunk == HEAD
    d_b = n_heads_b * HEAD
    d_a = d_in - 2 * d_b
    n_exp = router_w.shape[2]
    d_ff = w_down.shape[2]
    alpha = (2.0 * depth) ** 0.25
    vec = lambda p: p.reshape(1, -1)

    w_in_b = w_in[0].astype(BF16)
    w_o_b = w_o[0].astype(BF16)
    sgu_w_b = sgu_w[0].astype(BF16)
    sgu_bias = jnp.broadcast_to(sgu_b[0][:, :, None], (n_heads_b, HEAD, HEAD))
    cs = _channel_dft_table()
    rw_hi = router_w[0].astype(BF16)
    rw_lo = (router_w[0] - rw_hi.astype(F32)).astype(BF16)
    w_g, w_l = _split_gu(w_gu[0].reshape(n_exp * d, 2 * d_ff))
    w_g = w_g.reshape(n_exp, d, d_ff)
    w_l = w_l.reshape(n_exp, d, d_ff)
    b_g = b_gu[0][:, None, 0::2]
    b_l = b_gu[0][:, None, 1::2]
    w_d = _cast_bf16(w_down[0].reshape(n_exp * d_ff, d)).reshape(n_exp, d_ff, d)
    b_d = b_down[0][:, None, :]

    requests = (x_prompt, x_sample)
    total_rows = sum(x.shape[0] * x.shape[1] for x in requests)
    x1r = jnp.zeros((total_rows * (d // LANES), LANES), F32)
    logit_parts, row0 = [], 0
    for x in requests:
        n_batch, seq, _ = x.shape
        x2d = x.reshape(n_batch * seq, d)
        a, q, yb = _mix_in(x2d, vec(ln_in_g), vec(ln_in_b), w_in_b, cs, vec(sgu_norm_g[0]),
                           vec(sgu_norm_b[0]), sgu_w_b, sgu_bias, d_a, d_b)
        ya_lo, ya_hi = _seq_dft(a.reshape(n_batch, seq, d_a), q.reshape(n_batch, seq, d_a))
        x1r, logits = _mix_out(x2d, ya_lo, ya_hi, yb, vec(ln_in_g), vec(ln_in_b), w_o_b,
                               vec(ln1_g[0]), vec(ln1_b[0]), rw_hi, rw_lo, vec(router_b[0]),
                               alpha, row0, x1r)
        logit_parts.append(logits)
        row0 += n_batch * seq

    bm = min(MOE_ROWS, total_rows * TOP_K)
    gates, src, dst, blk_e, n_valid = _route(jnp.concatenate(logit_parts, axis=0), n_exp, bm)
    y_rows = _moe(blk_e, n_valid, src, dst, x1r, w_g, w_l, b_g, b_l, w_d, b_d,
                  total_rows * TOP_K + n_exp * bm)

    outs, row0 = [], 0
    for x in requests:
        rows = x.shape[0] * x.shape[1]
        out = _ln_out(x1r, y_rows, gates, vec(ln2_g[0]), vec(ln2_b[0]), alpha,
                      rows, row0, total_rows)
        outs.append(out.reshape(x.shape))
        row0 += rows
    return tuple(outs)
```

```python
import functools
import math

import jax
import jax.numpy as jnp
import numpy as np
from jax import lax
from jax.experimental import pallas as pl
from jax.experimental.pallas import tpu as pltpu

F32 = jnp.float32
BF16 = jnp.bfloat16

LN_EPS = 1e-5
TOP_K = 4
HEAD = 128
LANES = 128
SWIGLU_LIMIT = 7.0
SWIGLU_ALPHA = 1.702
VMEM_LIMIT_BYTES = 56 * 1024 * 1024

ROW_TILE = 256
DFT_K_TILE = 512
DFT_S_TILE = 512
DFT_C_TILE = 256
MOE_ROWS = 512
MOE_FF_TILE = 512
MOE_OUT_TILE = 512
MOE_ROW_PAD = 8
PREP_ROWS = 512
SPLIT_GROUP = 512
PREP_STREAMS = 4


def _layer_norm(x, g, b):
    mu = jnp.mean(x, axis=-1, keepdims=True)
    xc = x - mu
    var = jnp.mean(xc * xc, axis=-1, keepdims=True)
    return xc * lax.rsqrt(var + LN_EPS) * g + b


def _gelu(z):
    return 0.5 * z * (1.0 + lax.erf(z * (1.0 / math.sqrt(2.0))))


def _dot(a, b):
    return jnp.dot(a, b, preferred_element_type=F32)


def _params(semantics):
    return pltpu.CompilerParams(dimension_semantics=semantics,
                                vmem_limit_bytes=VMEM_LIMIT_BYTES)


def _whole(shape):
    return pl.BlockSpec(shape, lambda *_: (0,) * len(shape),
                        pipeline_mode=pl.Buffered(1))


def _store_token_rows(ref, val):
    rpt = val.shape[1] // LANES
    for c in range(rpt):
        ref[pl.ds(c, val.shape[0], stride=rpt), :] = val[:, c * LANES:(c + 1) * LANES]


def _load_token_rows(ref, n_tokens, rpt, pitch=None):
    return [ref[pl.ds(c, n_tokens, stride=pitch or rpt), :] for c in range(rpt)]


def _mix_in_body(x_ref, g_ref, b_ref, win_ref, cs_ref, sg_ref, sb_ref, sw_ref, sbias_ref,
                 a_ref, q_ref, yb_ref, *, d_a, d_b):
    xb = _layer_norm(x_ref[...], g_ref[...], b_ref[...]).astype(BF16)
    rows = xb.shape[0]

    za = _dot(xb, win_ref[:, :d_a]).astype(BF16)
    for h in range(d_a // HEAD):
        hs = slice(h * HEAD, (h + 1) * HEAD)
        ab = _dot(za[:, hs], cs_ref[...])
        a_ref[:, hs] = ab[:, :HEAD].astype(BF16)
        q_ref[:, hs] = ab[:, HEAD:].astype(BF16)

    u = _gelu(_dot(xb, win_ref[:, d_a:d_a + d_b]))
    v = _gelu(_dot(xb, win_ref[:, d_a + d_b:]))
    for h in range(d_b // HEAD):
        hs = slice(h * HEAD, (h + 1) * HEAD)
        vn = _layer_norm(v[:, hs], sg_ref[:, hs], sb_ref[:, hs]).astype(BF16)
        for c in range(rows // HEAD):
            rs = slice(c * HEAD, (c + 1) * HEAD)
            gate = _dot(sw_ref[h], vn[rs, :]) + sbias_ref[h]
            yb_ref[rs, hs] = (u[rs, hs] * gate).astype(BF16)


def _mix_in(x2d, ln_g, ln_b, w_in, cs, sgu_g, sgu_b, sgu_w, sgu_bias, d_a, d_b):
    rows, d = x2d.shape
    tm = min(ROW_TILE, rows)
    n_heads_b = d_b // HEAD
    row = lambda w: pl.BlockSpec((tm, w), lambda i: (i, 0))
    return pl.pallas_call(
        functools.partial(_mix_in_body, d_a=d_a, d_b=d_b),
        grid=(rows // tm,),
        in_specs=[row(d), _whole((1, d)), _whole((1, d)), _whole(w_in.shape),
                  _whole(cs.shape), _whole((1, d_b)), _whole((1, d_b)),
                  _whole((n_heads_b, HEAD, HEAD)), _whole((n_heads_b, HEAD, HEAD))],
        out_specs=[row(d_a), row(d_a), row(d_b)],
        out_shape=[jax.ShapeDtypeStruct((rows, d_a), BF16),
                   jax.ShapeDtypeStruct((rows, d_a), BF16),
                   jax.ShapeDtypeStruct((rows, d_b), BF16)],
        compiler_params=_params(("parallel",)),
        name="mix_in",
    )(x2d, ln_g, ln_b, w_in, cs, sgu_g, sgu_b, sgu_w, sgu_bias)


def _seq_dft_body(a_ref, q_ref, c0_ref, s0_ref, perm_ref, lo_ref, hi_ref,
                  accp_ref, accz_ref, cb_ref, sb_ref, ca_ref, sa_ref, cd_ref, sd_ref,
                  *, seq, scale):
    i, j = pl.program_id(1), pl.program_id(2)
    n_batch, tk, _ = lo_ref.shape
    tkx, ts = cb_ref.shape
    w = 2.0 * math.pi / seq

    @pl.when(j == 0)
    def _():
        accp_ref[...] = jnp.zeros_like(accp_ref)
        accz_ref[...] = jnp.zeros_like(accz_ref)
        ds = lax.broadcasted_iota(jnp.int32, (1, ts), 1)
        ph = (((i * tk) * ds) & (seq - 1)).astype(F32) * w
        ci, si = jnp.cos(ph), jnp.sin(ph)
        cb_ref[...] = ci * c0_ref[...] - si * s0_ref[...]
        sb_ref[...] = si * c0_ref[...] + ci * s0_ref[...]
        k = i * tk + lax.broadcasted_iota(jnp.int32, (tkx, HEAD), 0)
        pd = ((k * ts) & (seq - 1)).astype(F32) * w
        cd_ref[...] = jnp.cos(pd)
        sd_ref[...] = jnp.sin(pd)
        ca_ref[...] = jnp.ones_like(ca_ref)
        sa_ref[...] = jnp.zeros_like(sa_ref)

    ca, sa = ca_ref[...], sa_ref[...]
    cos_parts, sin_parts = [], []
    for g in range(ts // HEAD):
        gs = slice(g * HEAD, (g + 1) * HEAD)
        cb, sb = cb_ref[:, gs], sb_ref[:, gs]
        cos_parts.append((ca * cb - sa * sb).astype(BF16))
        sin_parts.append((sa * cb + ca * sb).astype(BF16))
    cos_t = jnp.concatenate(cos_parts, axis=1)
    sin_t = jnp.concatenate(sin_parts, axis=1)
    for b in range(n_batch):
        accp_ref[b] += _dot(cos_t, a_ref[b])
        accz_ref[b] += _dot(sin_t, q_ref[b])
    cd, sd = cd_ref[...], sd_ref[...]
    ca_ref[...] = ca * cd - sa * sd
    sa_ref[...] = sa * cd + ca * sd

    @pl.when(j == pl.num_programs(2) - 1)
    def _():
        first_row = lax.broadcasted_iota(jnp.int32, (tk, 1), 0) == 0
        for b in range(n_batch):
            p, z = accp_ref[b], accz_ref[b]
            lo_ref[b] = ((p[:tk] - z[:tk]) * scale).astype(BF16)
            mirror = ((p + z) * scale).astype(BF16)
            rev = _dot(perm_ref[...], mirror[:tk]).astype(BF16)
            hi_ref[b] = jnp.where(first_row, mirror[tk:tk + 1], rev)


def _seq_dft(a3d, q3d):
    n_batch, seq, d_a = a3d.shape
    assert seq & (seq - 1) == 0, "sequence length must be a power of two"
    half = seq // 2
    tk, ts, tc = min(DFT_K_TILE, half), min(DFT_S_TILE, seq), min(DFT_C_TILE, d_a)
    tkx = tk + 8
    n_i = half // tk
    scale = 1.0 / math.sqrt(seq * HEAD)
    ang = 2.0 * np.pi * np.outer(np.arange(tkx), np.arange(ts)) / seq
    c0 = jnp.asarray(np.cos(ang), dtype=F32)
    s0 = jnp.asarray(np.sin(ang), dtype=F32)
    perm = np.zeros((tk, tk), np.float32)
    r = np.arange(1, tk)
    perm[r, tk - r] = 1.0
    in_spec = pl.BlockSpec((n_batch, ts, tc), lambda c, i, j: (0, j, c))
    out = jax.ShapeDtypeStruct((n_batch, half, d_a), BF16)
    return pl.pallas_call(
        functools.partial(_seq_dft_body, seq=seq, scale=scale),
        grid=(d_a // tc, n_i, seq // ts),
        in_specs=[in_spec, in_spec, _whole((tkx, ts)), _whole((tkx, ts)), _whole((tk, tk))],
        out_specs=[pl.BlockSpec((n_batch, tk, tc), lambda c, i, j: (0, i, c)),
                   pl.BlockSpec((n_batch, tk, tc), lambda c, i, j: (0, n_i - 1 - i, c))],
        out_shape=[out, out],
        scratch_shapes=[pltpu.VMEM((n_batch, tkx, tc), F32), pltpu.VMEM((n_batch, tkx, tc), F32),
                        pltpu.VMEM((tkx, ts), F32), pltpu.VMEM((tkx, ts), F32)]
                       + [pltpu.VMEM((tkx, HEAD), F32)] * 4,
        compiler_params=_params(("parallel", "parallel", "arbitrary")),
        name="seq_dft",
    )(a3d, q3d, c0, s0, jnp.asarray(perm, dtype=BF16))


def _mix_out_body(x_ref, lo_ref, hi_ref, yb_ref, g0_ref, b0_ref, wo_ref, g1_ref, b1_ref,
                  rwh_ref, rwl_ref, rb_ref, x1r_in_ref, x1r_ref, lg_ref, *, alpha, d_a):
    del x1r_in_ref
    xln = _layer_norm(x_ref[...], g0_ref[...], b0_ref[...])
    in_low_half = pl.program_id(1) < pl.num_programs(1) // 2
    ya = jnp.where(in_low_half, lo_ref[...], hi_ref[...])
    mix = _dot(ya, wo_ref[:d_a, :]) + _dot(yb_ref[...], wo_ref[d_a:, :])
    x1 = _layer_norm(alpha * xln + mix, g1_ref[...], b1_ref[...])
    _store_token_rows(x1r_ref, x1)
    hi = x1.astype(BF16)
    lo = (x1 - hi.astype(F32)).astype(BF16)
    lg_ref[...] = (_dot(hi, rwh_ref[...]) + _dot(hi, rwl_ref[...])
                   + _dot(lo, rwh_ref[...]) + rb_ref[...])


def _mix_out(x2d, ya_lo, ya_hi, yb, g0, b0, w_o, g1, b1, rw_hi, rw_lo, rb, alpha, row0, x1r_all):
    rows, d = x2d.shape
    n_batch, half, d_a = ya_lo.shape
    n_exp = rw_hi.shape[1]
    tm = min(ROW_TILE, half)
    n_p = 2 * half // tm
    rpt = d // LANES
    assert row0 % tm == 0
    row = lambda w: pl.BlockSpec((tm, w), lambda b, p: (b * n_p + p, 0))
    vec = _whole((1, d))
    return pl.pallas_call(
        functools.partial(_mix_out_body, alpha=alpha, d_a=d_a),
        grid=(n_batch, n_p),
        in_specs=[row(d),
                  pl.BlockSpec((None, tm, d_a), lambda b, p: (b, jnp.minimum(p, n_p // 2 - 1), 0)),
                  pl.BlockSpec((None, tm, d_a), lambda b, p: (b, jnp.maximum(p - n_p // 2, 0), 0)),
                  row(yb.shape[1]), vec, vec, _whole(w_o.shape), vec, vec,
                  _whole(rw_hi.shape), _whole(rw_lo.shape), _whole((1, n_exp)),
                  pl.BlockSpec(memory_space=pl.ANY)],
        out_specs=[pl.BlockSpec((tm * rpt, LANES), lambda b, p: (row0 // tm + b * n_p + p, 0)),
                   row(n_exp)],
        out_shape=[jax.ShapeDtypeStruct(x1r_all.shape, F32),
                   jax.ShapeDtypeStruct((rows, n_exp), F32)],
        input_output_aliases={12: 0},
        compiler_params=_params(("parallel", "parallel")),
        name="mix_out",
    )(x2d, ya_lo, ya_hi, yb, g0, b0, w_o, g1, b1, rw_hi, rw_lo, rb, x1r_all)


def _column_streams(w2d, tr):
    rows, cols = w2d.shape
    n = PREP_STREAMS if cols % (PREP_STREAMS * SPLIT_GROUP) == 0 else 1
    specs = [pl.BlockSpec((tr, cols // n), functools.partial(lambda i, q: (i, q), q=q))
             for q in range(n)]
    return [w2d] * n, specs


def _split_gu_body(*refs):
    *w_refs, perm_ref, g_ref, l_ref = refs
    half = SPLIT_GROUP // 2
    groups = w_refs[0].shape[1] // SPLIT_GROUP
    for q, w_ref in enumerate(w_refs):
        for c in range(groups):
            w = w_ref[:, c * SPLIT_GROUP:(c + 1) * SPLIT_GROUP].astype(BF16)
            r = _dot(w, perm_ref[...])
            o = (q * groups + c) * half
            g_ref[:, o:o + half] = r[:, :half].astype(BF16)
            l_ref[:, o:o + half] = r[:, half:].astype(BF16)


def _split_gu(w2d):
    rows, cols = w2d.shape
    assert cols % SPLIT_GROUP == 0
    tr = min(PREP_ROWS, rows)
    perm = np.zeros((SPLIT_GROUP, SPLIT_GROUP), np.float32)
    j = np.arange(SPLIT_GROUP // 2)
    perm[2 * j, j] = 1.0
    perm[2 * j + 1, SPLIT_GROUP // 2 + j] = 1.0
    streams, specs = _column_streams(w2d, tr)
    out = jax.ShapeDtypeStruct((rows, cols // 2), BF16)
    return pl.pallas_call(
        _split_gu_body,
        grid=(rows // tr,),
        in_specs=specs + [_whole(perm.shape)],
        out_specs=[pl.BlockSpec((tr, cols // 2), lambda i: (i, 0))] * 2,
        out_shape=[out, out],
        compiler_params=_params(("parallel",)),
        name="split_gu",
    )(*streams, jnp.asarray(perm, dtype=BF16))


def _cast_body(*refs):
    *w_refs, o_ref = refs
    width = w_refs[0].shape[1]
    for q, w_ref in enumerate(w_refs):
        o_ref[:, q * width:(q + 1) * width] = w_ref[...].astype(BF16)


def _cast_bf16(w2d):
    rows, cols = w2d.shape
    tr = min(PREP_ROWS, rows)
    streams, specs = _column_streams(w2d, tr)
    return pl.pallas_call(
        _cast_body,
        grid=(rows // tr,),
        in_specs=specs,
        out_specs=pl.BlockSpec((tr, cols), lambda i: (i, 0)),
        out_shape=jax.ShapeDtypeStruct((rows, cols), BF16),
        compiler_params=_params(("parallel",)),
        name="cast_bf16",
    )(*streams)


def _moe_body(be_ref, nv_ref, src_ref, src_next_ref, dst_ref, dst_prev_ref, x_hbm, wg_ref,
              wl_ref, bg_ref, bl_ref, wd_ref, bd_ref, y_hbm, xbuf_ref, xs_ref, act_ref, obuf_ref,
              gsem, ssem):
    del be_ref
    i, n_valid = pl.program_id(0), nv_ref[0]
    bm, d = xs_ref.shape
    rpt = d // LANES
    pitch = obuf_ref.shape[0] // bm
    slot = i % 2

    def gather_copy(idx_ref, r, to_slot):
        tok = idx_ref[0, 0, r]
        return pltpu.make_async_copy(x_hbm.at[pl.ds(pl.multiple_of(tok * rpt, rpt), rpt), :],
                                     xbuf_ref.at[to_slot, pl.ds(r * pitch, rpt), :], gsem)

    def scatter_copy(idx_ref, r):
        row = idx_ref[0, 0, r]
        return pltpu.make_async_copy(obuf_ref.at[pl.ds(r * pitch, rpt), :],
                                     y_hbm.at[pl.ds(pl.multiple_of(row * rpt, rpt), rpt), :], ssem)

    def wait_gather(of_slot):
        pltpu.make_async_copy(x_hbm.at[pl.ds(0, bm * rpt), :],
                              xbuf_ref.at[of_slot, pl.ds(0, bm * rpt), :], gsem).wait()

    def wait_scatter():
        pltpu.make_async_copy(obuf_ref.at[pl.ds(0, bm * rpt), :],
                              y_hbm.at[pl.ds(0, bm * rpt), :], ssem).wait()

    @pl.when(i < n_valid)
    def _():
        @pl.when(i == 0)
        def _():
            obuf_ref[...] = jnp.zeros_like(obuf_ref)
            for r in range(bm):
                gather_copy(src_ref, r, 0).start(priority=r % 2)

        wait_gather(slot)
        for c, part in enumerate(_load_token_rows(xbuf_ref.at[slot], bm, rpt, pitch)):
            xs_ref[:, c * LANES:(c + 1) * LANES] = part.astype(BF16)
        x = xs_ref[...]
        d_ff = wg_ref.shape[2]
        tf = min(MOE_FF_TILE, d_ff)
        n_ff = d_ff // tf
        for f in range(n_ff):
            fs = slice(f * tf, (f + 1) * tf)
            glu = jnp.minimum(_dot(x, wg_ref[0, :, fs]) + bg_ref[0, :, fs], SWIGLU_LIMIT)
            lin = jnp.clip(_dot(x, wl_ref[0, :, fs]) + bl_ref[0, :, fs],
                           -SWIGLU_LIMIT, SWIGLU_LIMIT)
            for r in range(f * bm // n_ff, (f + 1) * bm // n_ff):
                gather_copy(src_next_ref, r, 1 - slot).start(priority=r % 2)
                scatter_copy(dst_prev_ref, r).start(priority=r % 2)
            act = glu * (1.0 / (1.0 + jnp.exp(-SWIGLU_ALPHA * glu))) * (lin + 1.0)
            act_ref[:, fs] = act.astype(BF16)

        wait_scatter()
        tn = min(MOE_OUT_TILE, d)
        for n in range(d // tn):
            ns = slice(n * tn, (n + 1) * tn)
            done = _dot(act_ref[...], wd_ref[0, :, ns]) + bd_ref[0, :, ns]
            for c in range(tn // LANES):
                obuf_ref[pl.ds(n * (tn // LANES) + c, bm, stride=pitch), :] = (
                    done[:, c * LANES:(c + 1) * LANES])

        @pl.when(i == n_valid - 1)
        def _():
            for r in range(bm):
                scatter_copy(dst_ref, r).start(priority=r % 2)
            wait_gather(1 - slot)
            wait_scatter()


def _moe(blk_e, n_valid, src, dst, x1r, w_g, w_l, b_g, b_l, w_d, b_d, n_out_rows):
    n_blocks, _, bm = src.shape
    n_exp, d, d_ff = w_g.shape
    rpt = d // LANES
    pitch = rpt + MOE_ROW_PAD if rpt % 16 == 0 else rpt
    expert = lambda i, be, nv: (be[i], 0, 0)
    single = dict(pipeline_mode=pl.Buffered(1))
    idx_block = lambda index_map: pl.BlockSpec((1, 1, bm), index_map, memory_space=pltpu.SMEM)
    grid_spec = pltpu.PrefetchScalarGridSpec(
        num_scalar_prefetch=2,
        grid=(n_blocks,),
        in_specs=[
            idx_block(lambda i, be, nv: (i, 0, 0)),
            idx_block(lambda i, be, nv: (jnp.minimum(i + 1, n_blocks - 1), 0, 0)),
            idx_block(lambda i, be, nv: (i, 0, 0)),
            idx_block(lambda i, be, nv: (jnp.maximum(i - 1, 0), 0, 0)),
            pl.BlockSpec(memory_space=pl.ANY),
            pl.BlockSpec((1, d, d_ff), expert, **single),
            pl.BlockSpec((1, d, d_ff), expert, **single),
            pl.BlockSpec((1, 1, d_ff), expert),
            pl.BlockSpec((1, 1, d_ff), expert),
            pl.BlockSpec((1, d_ff, d), expert, **single),
            pl.BlockSpec((1, 1, d), expert),
        ],
        out_specs=pl.BlockSpec(memory_space=pl.ANY),
        scratch_shapes=[pltpu.VMEM((2, bm * pitch, LANES), F32), pltpu.VMEM((bm, d), BF16),
                        pltpu.VMEM((bm, d_ff), BF16), pltpu.VMEM((bm * pitch, LANES), F32),
                        pltpu.SemaphoreType.DMA(()), pltpu.SemaphoreType.DMA(())],
    )
    return pl.pallas_call(
        _moe_body,
        grid_spec=grid_spec,
        out_shape=jax.ShapeDtypeStruct((n_out_rows * rpt, LANES), F32),
        compiler_params=_params(("arbitrary",)),
        name="moe",
    )(blk_e, n_valid, src, src, dst, dst, x1r, w_g, w_l, b_g, b_l, w_d, b_d)


def _ln_out_body(x1r_ref, gate_ref, *refs, alpha):
    *y_refs, g_ref, b_ref, o_ref = refs
    tm, d = o_ref.shape
    rpt = d // LANES
    gates = gate_ref[...]
    parts = [alpha * p for p in _load_token_rows(x1r_ref, tm, rpt)]
    for c, y_ref in enumerate(y_refs):
        gate = jnp.broadcast_to(gates[:, c:c + 1], (tm, LANES))
        parts = [p + gate * y for p, y in zip(parts, _load_token_rows(y_ref, tm, rpt))]
    o_ref[...] = _layer_norm(jnp.concatenate(parts, axis=1), g_ref[...], b_ref[...])


def _ln_out(x1r, y_rows, gates, g, b, alpha, rows, row0, total_rows):
    d = g.shape[1]
    rpt = d // LANES
    tm = min(ROW_TILE, rows)
    assert row0 % tm == 0 and total_rows % tm == 0
    tile0 = row0 // tm
    tok = lambda off: pl.BlockSpec((tm * rpt, LANES), lambda i: (off + tile0 + i, 0))
    return pl.pallas_call(
        functools.partial(_ln_out_body, alpha=alpha),
        grid=(rows // tm,),
        in_specs=[tok(0), pl.BlockSpec((tm, TOP_K), lambda i: (tile0 + i, 0))]
                 + [tok(c * (total_rows // tm)) for c in range(TOP_K)]
                 + [_whole((1, d)), _whole((1, d))],
        out_specs=pl.BlockSpec((tm, d), lambda i: (i, 0)),
        out_shape=jax.ShapeDtypeStruct((rows, d), F32),
        compiler_params=_params(("parallel",)),
        name="ln_out",
    )(x1r, gates, *([y_rows] * TOP_K), g, b)


def _route(logits, n_exp, bm):
    t = logits.shape[0]
    tk = t * TOP_K
    top_vals, top_idx = lax.top_k(logits, TOP_K)
    gates = jax.nn.softmax(top_vals, axis=-1)
    flat_e = top_idx.reshape(-1).astype(jnp.int32)
    sorted_e, order = lax.sort_key_val(flat_e, jnp.arange(tk, dtype=jnp.int32))
    experts = jnp.arange(n_exp, dtype=jnp.int32)
    starts = jnp.searchsorted(sorted_e, experts, side='left').astype(jnp.int32)
    counts = jnp.searchsorted(sorted_e, experts, side='right').astype(jnp.int32) - starts
    padded = ((counts + bm - 1) // bm) * bm
    pends = jnp.cumsum(padded)
    pstarts = pends - padded
    n_blocks = (tk + n_exp * bm) // bm
    blk_start = jnp.arange(n_blocks, dtype=jnp.int32) * bm
    blk_e = jnp.minimum(jnp.searchsorted(pends, blk_start, side='right'),
                        n_exp - 1).astype(jnp.int32)
    n_valid = (pends[-1:] // bm).astype(jnp.int32)
    lane = jnp.arange(bm, dtype=jnp.int32)[None, :]
    row_in_expert = (blk_start - pstarts[blk_e])[:, None] + lane
    valid = row_in_expert < counts[blk_e][:, None]
    pick = order[jnp.clip(starts[blk_e][:, None] + row_in_expert, 0, tk - 1)]
    tok, choice = pick // TOP_K, pick % TOP_K
    src = jnp.where(valid, tok, 0)
    spare = tk + blk_start[:, None] + lane - (starts + counts)[blk_e][:, None]
    dst = jnp.where(valid, choice * t + tok, spare)
    return gates, src[:, None, :], dst[:, None, :], blk_e, n_valid


def _channel_dft_table():
    n = np.arange(HEAD)
    ang = 2.0 * np.pi * ((n[:, None] * n[None, :]) % HEAD) / HEAD
    return jnp.asarray(np.concatenate([np.cos(ang), np.sin(ang)], axis=1), dtype=BF16)


def kernel(x_prompt, x_sample, ln_in_g, ln_in_b, w_in, sgu_norm_g, sgu_norm_b, sgu_w, sgu_b,
           w_o, ln1_g, ln1_b, router_w, router_b, w_gu, b_gu, w_down, b_down, ln2_g, ln2_b):
    depth, d, d_in = w_in.shape
    assert depth == 1, "single-layer encoder only"
    n_heads_b, chunk, _ = sgu_w.shape[1:]
    assert chunk == HEAD
    d_b = n_heads_b * HEAD
    d_a = d_in - 2 * d_b
    n_exp = router_w.shape[2]
    d_ff = w_down.shape[2]
    alpha = (2.0 * depth) ** 0.25
    vec = lambda p: p.reshape(1, -1)

    w_in_b = w_in[0].astype(BF16)
    w_o_b = w_o[0].astype(BF16)
    sgu_w_b = sgu_w[0].astype(BF16)
    sgu_bias = jnp.broadcast_to(sgu_b[0][:, :, None], (n_heads_b, HEAD, HEAD))
    cs = _channel_dft_table()
    rw_hi = router_w[0].astype(BF16)
    rw_lo = (router_w[0] - rw_hi.astype(F32)).astype(BF16)
    w_g, w_l = _split_gu(w_gu[0].reshape(n_exp * d, 2 * d_ff))
    w_g = w_g.reshape(n_exp, d, d_ff)
    w_l = w_l.reshape(n_exp, d, d_ff)
    b_g = b_gu[0][:, None, 0::2]
    b_l = b_gu[0][:, None, 1::2]
    w_d = _cast_bf16(w_down[0].reshape(n_exp * d_ff, d)).reshape(n_exp, d_ff, d)
    b_d = b_down[0][:, None, :]

    requests = (x_prompt, x_sample)
    total_rows = sum(x.shape[0] * x.shape[1] for x in requests)
    x1r = jnp.zeros((total_rows * (d // LANES), LANES), F32)
    logit_parts, row0 = [], 0
    for x in requests:
        n_batch, seq, _ = x.shape
        x2d = x.reshape(n_batch * seq, d)
        a, q, yb = _mix_in(x2d, vec(ln_in_g), vec(ln_in_b), w_in_b, cs, vec(sgu_norm_g[0]),
                           vec(sgu_norm_b[0]), sgu_w_b, sgu_bias, d_a, d_b)
        ya_lo, ya_hi = _seq_dft(a.reshape(n_batch, seq, d_a), q.reshape(n_batch, seq, d_a))
        x1r, logits = _mix_out(x2d, ya_lo, ya_hi, yb, vec(ln_in_g), vec(ln_in_b), w_o_b,
                               vec(ln1_g[0]), vec(ln1_b[0]), rw_hi, rw_lo, vec(router_b[0]),
                               alpha, row0, x1r)
        logit_parts.append(logits)
        row0 += n_batch * seq

    bm = min(MOE_ROWS, total_rows * TOP_K)
    gates, src, dst, blk_e, n_valid = _route(jnp.concatenate(logit_parts, axis=0), n_exp, bm)
    y_rows = _moe(blk_e, n_valid, src, dst, x1r, w_g, w_l, b_g, b_l, w_d, b_d,
                  total_rows * TOP_K + n_exp * bm)

    outs, row0 = [], 0
    for x in requests:
        rows = x.shape[0] * x.shape[1]
        out = _ln_out(x1r, y_rows, gates, vec(ln2_g[0]), vec(ln2_b[0]), alpha,
                      rows, row0, total_rows)
        outs.append(out.reshape(x.shape))
        row0 += rows
    return tuple(outs)
```

```python
import functools
import math

import jax
import jax.numpy as jnp
import numpy as np
from jax import lax
from jax.experimental import pallas as pl
from jax.experimental.pallas import tpu as pltpu

F32 = jnp.float32
BF16 = jnp.bfloat16

LN_EPS = 1e-5
TOP_K = 4
HEAD = 128
LANES = 128
SWIGLU_LIMIT = 7.0
SWIGLU_ALPHA = 1.702
VMEM_LIMIT_BYTES = 56 * 1024 * 1024

ROW_TILE = 256
MIX_IN_ROWS = 512
DFT_K_TILE = 512
DFT_S_TILE = 512
DFT_C_TILE = 256
MOE_ROWS = 512
MOE_FF_TILE = 512
MOE_OUT_TILE = 512
MOE_ROW_PAD = 8
PREP_ROWS = 512
SPLIT_GROUP = 512
PREP_STREAMS = 4


def _layer_norm(x, g, b):
    mu = jnp.mean(x, axis=-1, keepdims=True)
    xc = x - mu
    var = jnp.mean(xc * xc, axis=-1, keepdims=True)
    return xc * lax.rsqrt(var + LN_EPS) * g + b


def _gelu(z):
    return 0.5 * z * (1.0 + lax.erf(z * (1.0 / math.sqrt(2.0))))


def _dot(a, b):
    return jnp.dot(a, b, preferred_element_type=F32)


def _params(semantics):
    return pltpu.CompilerParams(dimension_semantics=semantics,
                                vmem_limit_bytes=VMEM_LIMIT_BYTES)


def _whole(shape):
    return pl.BlockSpec(shape, lambda *_: (0,) * len(shape),
                        pipeline_mode=pl.Buffered(1))


def _store_token_rows(ref, val):
    rpt = val.shape[1] // LANES
    for c in range(rpt):
        ref[pl.ds(c, val.shape[0], stride=rpt), :] = val[:, c * LANES:(c + 1) * LANES]


def _load_token_rows(ref, n_tokens, rpt, pitch=None):
    return [ref[pl.ds(c, n_tokens, stride=pitch or rpt), :] for c in range(rpt)]


def _mix_in_body(x_ref, g_ref, b_ref, win_ref, cs_ref, sg_ref, sb_ref, sw_ref, sbias_ref,
                 a_ref, q_ref, yb_ref, *, d_a, d_b):
    xb = _layer_norm(x_ref[...], g_ref[...], b_ref[...]).astype(BF16)
    rows = xb.shape[0]

    za = _dot(xb, win_ref[:, :d_a]).astype(BF16)
    for h in range(d_a // HEAD):
        hs = slice(h * HEAD, (h + 1) * HEAD)
        ab = _dot(za[:, hs], cs_ref[...])
        a_ref[:, hs] = ab[:, :HEAD].astype(BF16)
        q_ref[:, hs] = ab[:, HEAD:].astype(BF16)

    u = _gelu(_dot(xb, win_ref[:, d_a:d_a + d_b]))
    v = _gelu(_dot(xb, win_ref[:, d_a + d_b:]))
    for h in range(d_b // HEAD):
        hs = slice(h * HEAD, (h + 1) * HEAD)
        vn = _layer_norm(v[:, hs], sg_ref[:, hs], sb_ref[:, hs]).astype(BF16)
        for c in range(rows // HEAD):
            rs = slice(c * HEAD, (c + 1) * HEAD)
            gate = _dot(sw_ref[h], vn[rs, :]) + sbias_ref[h]
            yb_ref[rs, hs] = (u[rs, hs] * gate).astype(BF16)


def _mix_in(x2d, ln_g, ln_b, w_in, cs, sgu_g, sgu_b, sgu_w, sgu_bias, d_a, d_b):
    rows, d = x2d.shape
    tm = min(MIX_IN_ROWS, rows)
    n_heads_b = d_b // HEAD
    row = lambda w: pl.BlockSpec((tm, w), lambda i: (i, 0))
    return pl.pallas_call(
        functools.partial(_mix_in_body, d_a=d_a, d_b=d_b),
        grid=(rows // tm,),
        in_specs=[row(d), _whole((1, d)), _whole((1, d)), _whole(w_in.shape),
                  _whole(cs.shape), _whole((1, d_b)), _whole((1, d_b)),
                  _whole((n_heads_b, HEAD, HEAD)), _whole((n_heads_b, HEAD, HEAD))],
        out_specs=[row(d_a), row(d_a), row(d_b)],
        out_shape=[jax.ShapeDtypeStruct((rows, d_a), BF16),
                   jax.ShapeDtypeStruct((rows, d_a), BF16),
                   jax.ShapeDtypeStruct((rows, d_b), BF16)],
        compiler_params=_params(("parallel",)),
        name="mix_in",
    )(x2d, ln_g, ln_b, w_in, cs, sgu_g, sgu_b, sgu_w, sgu_bias)


def _seq_dft_body(a_ref, q_ref, c0_ref, s0_ref, perm_ref, lo_ref, hi_ref,
                  accp_ref, accz_ref, cb_ref, sb_ref, ca_ref, sa_ref, cd_ref, sd_ref,
                  *, seq, scale):
    i, j = pl.program_id(1), pl.program_id(2)
    n_batch, tk, _ = lo_ref.shape
    tkx, ts = cb_ref.shape
    w = 2.0 * math.pi / seq

    @pl.when(j == 0)
    def _():
        accp_ref[...] = jnp.zeros_like(accp_ref)
        accz_ref[...] = jnp.zeros_like(accz_ref)
        ds = lax.broadcasted_iota(jnp.int32, (1, ts), 1)
        ph = (((i * tk) * ds) & (seq - 1)).astype(F32) * w
        ci, si = jnp.cos(ph), jnp.sin(ph)
        cb_ref[...] = ci * c0_ref[...] - si * s0_ref[...]
        sb_ref[...] = si * c0_ref[...] + ci * s0_ref[...]
        k = i * tk + lax.broadcasted_iota(jnp.int32, (tkx, HEAD), 0)
        pd = ((k * ts) & (seq - 1)).astype(F32) * w
        cd_ref[...] = jnp.cos(pd)
        sd_ref[...] = jnp.sin(pd)
        ca_ref[...] = jnp.ones_like(ca_ref)
        sa_ref[...] = jnp.zeros_like(sa_ref)

    ca, sa = ca_ref[...], sa_ref[...]
    cos_parts, sin_parts = [], []
    for g in range(ts // HEAD):
        gs = slice(g * HEAD, (g + 1) * HEAD)
        cb, sb = cb_ref[:, gs], sb_ref[:, gs]
        cos_parts.append((ca * cb - sa * sb).astype(BF16))
        sin_parts.append((sa * cb + ca * sb).astype(BF16))
    cos_t = jnp.concatenate(cos_parts, axis=1)
    sin_t = jnp.concatenate(sin_parts, axis=1)
    for b in range(n_batch):
        accp_ref[b] += _dot(cos_t, a_ref[b])
        accz_ref[b] += _dot(sin_t, q_ref[b])
    cd, sd = cd_ref[...], sd_ref[...]
    ca_ref[...] = ca * cd - sa * sd
    sa_ref[...] = sa * cd + ca * sd

    @pl.when(j == pl.num_programs(2) - 1)
    def _():
        first_row = lax.broadcasted_iota(jnp.int32, (tk, 1), 0) == 0
        for b in range(n_batch):
            p, z = accp_ref[b], accz_ref[b]
            lo_ref[b] = ((p[:tk] - z[:tk]) * scale).astype(BF16)
            mirror = ((p + z) * scale).astype(BF16)
            rev = _dot(perm_ref[...], mirror[:tk]).astype(BF16)
            hi_ref[b] = jnp.where(first_row, mirror[tk:tk + 1], rev)


def _seq_dft(a3d, q3d):
    n_batch, seq, d_a = a3d.shape
    assert seq & (seq - 1) == 0, "sequence length must be a power of two"
    half = seq // 2
    tk, ts, tc = min(DFT_K_TILE, half), min(DFT_S_TILE, seq), min(DFT_C_TILE, d_a)
    tkx = tk + 8
    n_i = half // tk
    scale = 1.0 / math.sqrt(seq * HEAD)
    ang = 2.0 * np.pi * np.outer(np.arange(tkx), np.arange(ts)) / seq
    c0 = jnp.asarray(np.cos(ang), dtype=F32)
    s0 = jnp.asarray(np.sin(ang), dtype=F32)
    perm = np.zeros((tk, tk), np.float32)
    r = np.arange(1, tk)
    perm[r, tk - r] = 1.0
    in_spec = pl.BlockSpec((n_batch, ts, tc), lambda c, i, j: (0, j, c))
    out = jax.ShapeDtypeStruct((n_batch, half, d_a), BF16)
    return pl.pallas_call(
        functools.partial(_seq_dft_body, seq=seq, scale=scale),
        grid=(d_a // tc, n_i, seq // ts),
        in_specs=[in_spec, in_spec, _whole((tkx, ts)), _whole((tkx, ts)), _whole((tk, tk))],
        out_specs=[pl.BlockSpec((n_batch, tk, tc), lambda c, i, j: (0, i, c)),
                   pl.BlockSpec((n_batch, tk, tc), lambda c, i, j: (0, n_i - 1 - i, c))],
        out_shape=[out, out],
        scratch_shapes=[pltpu.VMEM((n_batch, tkx, tc), F32), pltpu.VMEM((n_batch, tkx, tc), F32),
                        pltpu.VMEM((tkx, ts), F32), pltpu.VMEM((tkx, ts), F32)]
                       + [pltpu.VMEM((tkx, HEAD), F32)] * 4,
        compiler_params=_params(("parallel", "parallel", "arbitrary")),
        name="seq_dft",
    )(a3d, q3d, c0, s0, jnp.asarray(perm, dtype=BF16))


def _top_k_gates(logits_t):
    n_exp, n_tok = logits_t.shape
    expert = lax.broadcasted_iota(jnp.int32, (n_exp, n_tok), 0)
    slot = lax.broadcasted_iota(jnp.int32, (TOP_K, n_tok), 0)
    work = logits_t
    top = jnp.zeros((TOP_K, n_tok), F32)
    idx = jnp.zeros((TOP_K, n_tok), jnp.int32)
    for c in range(TOP_K):
        best = jnp.max(work, axis=0, keepdims=True)
        where = jnp.min(jnp.where(work == best, expert, n_exp), axis=0, keepdims=True)
        top = jnp.where(slot == c, best, top)
        idx = jnp.where(slot == c, where, idx)
        work = jnp.where(expert == where, -jnp.inf, work)
    e = jnp.exp(top - jnp.max(top, axis=0, keepdims=True))
    return idx, e / jnp.sum(e, axis=0, keepdims=True)


def _mix_out_body(x_ref, lo_ref, hi_ref, yb_ref, g0_ref, b0_ref, wo_ref, g1_ref, b1_ref,
                  rwh_ref, rwl_ref, rb_ref, x1r_in_ref, x1r_ref, idx_ref, gate_ref, *, alpha, d_a):
    del x1r_in_ref
    xln = _layer_norm(x_ref[...], g0_ref[...], b0_ref[...])
    in_low_half = pl.program_id(1) < pl.num_programs(1) // 2
    ya = jnp.where(in_low_half, lo_ref[...], hi_ref[...])
    mix = _dot(ya, wo_ref[:d_a, :]) + _dot(yb_ref[...], wo_ref[d_a:, :])
    x1 = _layer_norm(alpha * xln + mix, g1_ref[...], b1_ref[...])
    _store_token_rows(x1r_ref, x1)
    hi = x1.astype(BF16)
    lo = (x1 - hi.astype(F32)).astype(BF16)
    logits = (_dot(hi, rwh_ref[...]) + _dot(hi, rwl_ref[...])
              + _dot(lo, rwh_ref[...]) + rb_ref[...])
    idx_ref[...], gate_ref[...] = _top_k_gates(logits.T)


def _mix_out(x2d, ya_lo, ya_hi, yb, g0, b0, w_o, g1, b1, rw_hi, rw_lo, rb, alpha, row0, x1r_all):
    rows, d = x2d.shape
    n_batch, half, d_a = ya_lo.shape
    n_exp = rw_hi.shape[1]
    tm = min(ROW_TILE, half)
    n_p = 2 * half // tm
    col = pl.BlockSpec((TOP_K, tm), lambda b, p: (0, b * n_p + p))
    rpt = d // LANES
    assert row0 % tm == 0
    row = lambda w: pl.BlockSpec((tm, w), lambda b, p: (b * n_p + p, 0))
    vec = _whole((1, d))
    return pl.pallas_call(
        functools.partial(_mix_out_body, alpha=alpha, d_a=d_a),
        grid=(n_batch, n_p),
        in_specs=[row(d),
                  pl.BlockSpec((None, tm, d_a), lambda b, p: (b, jnp.minimum(p, n_p // 2 - 1), 0)),
                  pl.BlockSpec((None, tm, d_a), lambda b, p: (b, jnp.maximum(p - n_p // 2, 0), 0)),
                  row(yb.shape[1]), vec, vec, _whole(w_o.shape), vec, vec,
                  _whole(rw_hi.shape), _whole(rw_lo.shape), _whole((1, n_exp)),
                  pl.BlockSpec(memory_space=pl.ANY)],
        out_specs=[pl.BlockSpec((tm * rpt, LANES), lambda b, p: (row0 // tm + b * n_p + p, 0)),
                   col, col],
        out_shape=[jax.ShapeDtypeStruct(x1r_all.shape, F32),
                   jax.ShapeDtypeStruct((TOP_K, rows), jnp.int32),
                   jax.ShapeDtypeStruct((TOP_K, rows), F32)],
        input_output_aliases={12: 0},
        compiler_params=_params(("parallel", "parallel")),
        name="mix_out",
    )(x2d, ya_lo, ya_hi, yb, g0, b0, w_o, g1, b1, rw_hi, rw_lo, rb, x1r_all)


def _column_streams(w2d, tr):
    rows, cols = w2d.shape
    n = PREP_STREAMS if cols % (PREP_STREAMS * SPLIT_GROUP) == 0 else 1
    specs = [pl.BlockSpec((tr, cols // n), functools.partial(lambda i, q: (i, q), q=q))
             for q in range(n)]
    return [w2d] * n, specs


def _split_gu_body(*refs):
    *w_refs, perm_ref, g_ref, l_ref = refs
    half = SPLIT_GROUP // 2
    groups = w_refs[0].shape[1] // SPLIT_GROUP
    for q, w_ref in enumerate(w_refs):
        for c in range(groups):
            w = w_ref[:, c * SPLIT_GROUP:(c + 1) * SPLIT_GROUP].astype(BF16)
            r = _dot(w, perm_ref[...])
            o = (q * groups + c) * half
            g_ref[:, o:o + half] = r[:, :half].astype(BF16)
            l_ref[:, o:o + half] = r[:, half:].astype(BF16)


def _split_gu(w2d):
    rows, cols = w2d.shape
    assert cols % SPLIT_GROUP == 0
    tr = min(PREP_ROWS, rows)
    perm = np.zeros((SPLIT_GROUP, SPLIT_GROUP), np.float32)
    j = np.arange(SPLIT_GROUP // 2)
    perm[2 * j, j] = 1.0
    perm[2 * j + 1, SPLIT_GROUP // 2 + j] = 1.0
    streams, specs = _column_streams(w2d, tr)
    out = jax.ShapeDtypeStruct((rows, cols // 2), BF16)
    return pl.pallas_call(
        _split_gu_body,
        grid=(rows // tr,),
        in_specs=specs + [_whole(perm.shape)],
        out_specs=[pl.BlockSpec((tr, cols // 2), lambda i: (i, 0))] * 2,
        out_shape=[out, out],
        compiler_params=_params(("parallel",)),
        name="split_gu",
    )(*streams, jnp.asarray(perm, dtype=BF16))


def _cast_body(*refs):
    *w_refs, o_ref = refs
    width = w_refs[0].shape[1]
    for q, w_ref in enumerate(w_refs):
        o_ref[:, q * width:(q + 1) * width] = w_ref[...].astype(BF16)


def _cast_bf16(w2d):
    rows, cols = w2d.shape
    tr = min(PREP_ROWS, rows)
    streams, specs = _column_streams(w2d, tr)
    return pl.pallas_call(
        _cast_body,
        grid=(rows // tr,),
        in_specs=specs,
        out_specs=pl.BlockSpec((tr, cols), lambda i: (i, 0)),
        out_shape=jax.ShapeDtypeStruct((rows, cols), BF16),
        compiler_params=_params(("parallel",)),
        name="cast_bf16",
    )(*streams)


def _moe_body(be_ref, nv_ref, src_ref, src_next_ref, dst_ref, dst_prev_ref, x_hbm, wg_ref,
              wl_ref, bg_ref, bl_ref, wd_ref, bd_ref, y_hbm, xbuf_ref, xs_ref, act_ref, obuf_ref,
              gsem, ssem):
    del be_ref
    i, n_valid = pl.program_id(0), nv_ref[0]
    bm, d = xs_ref.shape
    rpt = d // LANES
    pitch = obuf_ref.shape[0] // bm
    slot = i % 2

    def gather_copy(idx_ref, r, to_slot):
        tok = idx_ref[0, 0, r]
        return pltpu.make_async_copy(x_hbm.at[pl.ds(pl.multiple_of(tok * rpt, rpt), rpt), :],
                                     xbuf_ref.at[to_slot, pl.ds(r * pitch, rpt), :], gsem)

    def scatter_copy(idx_ref, r):
        row = idx_ref[0, 0, r]
        return pltpu.make_async_copy(obuf_ref.at[pl.ds(r * pitch, rpt), :],
                                     y_hbm.at[pl.ds(pl.multiple_of(row * rpt, rpt), rpt), :], ssem)

    def wait_gather(of_slot):
        pltpu.make_async_copy(x_hbm.at[pl.ds(0, bm * rpt), :],
                              xbuf_ref.at[of_slot, pl.ds(0, bm * rpt), :], gsem).wait()

    def wait_scatter():
        pltpu.make_async_copy(obuf_ref.at[pl.ds(0, bm * rpt), :],
                              y_hbm.at[pl.ds(0, bm * rpt), :], ssem).wait()

    @pl.when(i < n_valid)
    def _():
        @pl.when(i == 0)
        def _():
            obuf_ref[...] = jnp.zeros_like(obuf_ref)
            for r in range(bm):
                gather_copy(src_ref, r, 0).start(priority=r % 2)

        wait_gather(slot)
        for c, part in enumerate(_load_token_rows(xbuf_ref.at[slot], bm, rpt, pitch)):
            xs_ref[:, c * LANES:(c + 1) * LANES] = part.astype(BF16)
        x = xs_ref[...]
        d_ff = wg_ref.shape[2]
        tf = min(MOE_FF_TILE, d_ff)
        n_ff = d_ff // tf
        for f in range(n_ff):
            fs = slice(f * tf, (f + 1) * tf)
            glu = jnp.minimum(_dot(x, wg_ref[0, :, fs]) + bg_ref[0, :, fs], SWIGLU_LIMIT)
            lin = jnp.clip(_dot(x, wl_ref[0, :, fs]) + bl_ref[0, :, fs],
                           -SWIGLU_LIMIT, SWIGLU_LIMIT)
            for r in range(f * bm // n_ff, (f + 1) * bm // n_ff):
                gather_copy(src_next_ref, r, 1 - slot).start(priority=r % 2)
                scatter_copy(dst_prev_ref, r).start(priority=r % 2)
            act = glu * (1.0 / (1.0 + jnp.exp(-SWIGLU_ALPHA * glu))) * (lin + 1.0)
            act_ref[:, fs] = act.astype(BF16)

        wait_scatter()
        tn = min(MOE_OUT_TILE, d)
        for n in range(d // tn):
            ns = slice(n * tn, (n + 1) * tn)
            done = _dot(act_ref[...], wd_ref[0, :, ns]) + bd_ref[0, :, ns]
            for c in range(tn // LANES):
                obuf_ref[pl.ds(n * (tn // LANES) + c, bm, stride=pitch), :] = (
                    done[:, c * LANES:(c + 1) * LANES])

        @pl.when(i == n_valid - 1)
        def _():
            for r in range(bm):
                scatter_copy(dst_ref, r).start(priority=r % 2)
            wait_gather(1 - slot)
            wait_scatter()


def _moe(blk_e, n_valid, src, dst, x1r, w_g, w_l, b_g, b_l, w_d, b_d, n_out_rows):
    n_blocks, _, bm = src.shape
    n_exp, d, d_ff = w_g.shape
    rpt = d // LANES
    pitch = rpt + MOE_ROW_PAD if rpt % 16 == 0 else rpt
    expert = lambda i, be, nv: (be[i], 0, 0)
    single = dict(pipeline_mode=pl.Buffered(1))
    idx_block = lambda index_map: pl.BlockSpec((1, 1, bm), index_map, memory_space=pltpu.SMEM)
    grid_spec = pltpu.PrefetchScalarGridSpec(
        num_scalar_prefetch=2,
        grid=(n_blocks,),
        in_specs=[
            idx_block(lambda i, be, nv: (i, 0, 0)),
            idx_block(lambda i, be, nv: (jnp.minimum(i + 1, n_blocks - 1), 0, 0)),
            idx_block(lambda i, be, nv: (i, 0, 0)),
            idx_block(lambda i, be, nv: (jnp.maximum(i - 1, 0), 0, 0)),
            pl.BlockSpec(memory_space=pl.ANY),
            pl.BlockSpec((1, d, d_ff), expert, **single),
            pl.BlockSpec((1, d, d_ff), expert, **single),
            pl.BlockSpec((1, 1, d_ff), expert),
            pl.BlockSpec((1, 1, d_ff), expert),
            pl.BlockSpec((1, d_ff, d), expert, **single),
            pl.BlockSpec((1, 1, d), expert),
        ],
        out_specs=pl.BlockSpec(memory_space=pl.ANY),
        scratch_shapes=[pltpu.VMEM((2, bm * pitch, LANES), F32), pltpu.VMEM((bm, d), BF16),
                        pltpu.VMEM((bm, d_ff), BF16), pltpu.VMEM((bm * pitch, LANES), F32),
                        pltpu.SemaphoreType.DMA(()), pltpu.SemaphoreType.DMA(())],
    )
    return pl.pallas_call(
        _moe_body,
        grid_spec=grid_spec,
        out_shape=jax.ShapeDtypeStruct((n_out_rows * rpt, LANES), F32),
        compiler_params=_params(("arbitrary",)),
        name="moe",
    )(blk_e, n_valid, src, src, dst, dst, x1r, w_g, w_l, b_g, b_l, w_d, b_d)


def _ln_out_body(x1r_ref, gate_ref, *refs, alpha):
    *y_refs, g_ref, b_ref, o_ref = refs
    tm, d = o_ref.shape
    rpt = d // LANES
    gates = gate_ref[...]
    parts = [alpha * p for p in _load_token_rows(x1r_ref, tm, rpt)]
    for c, y_ref in enumerate(y_refs):
        gate = jnp.broadcast_to(gates[:, c:c + 1], (tm, LANES))
        parts = [p + gate * y for p, y in zip(parts, _load_token_rows(y_ref, tm, rpt))]
    o_ref[...] = _layer_norm(jnp.concatenate(parts, axis=1), g_ref[...], b_ref[...])


def _ln_out(x1r, y_rows, gates, g, b, alpha, rows, row0, total_rows):
    d = g.shape[1]
    rpt = d // LANES
    tm = min(ROW_TILE, rows)
    assert row0 % tm == 0 and total_rows % tm == 0
    tile0 = row0 // tm
    tok = lambda off: pl.BlockSpec((tm * rpt, LANES), lambda i: (off + tile0 + i, 0))
    return pl.pallas_call(
        functools.partial(_ln_out_body, alpha=alpha),
        grid=(rows // tm,),
        in_specs=[tok(0), pl.BlockSpec((tm, TOP_K), lambda i: (tile0 + i, 0))]
                 + [tok(c * (total_rows // tm)) for c in range(TOP_K)]
                 + [_whole((1, d)), _whole((1, d))],
        out_specs=pl.BlockSpec((tm, d), lambda i: (i, 0)),
        out_shape=jax.ShapeDtypeStruct((rows, d), F32),
        compiler_params=_params(("parallel",)),
        name="ln_out",
    )(x1r, gates, *([y_rows] * TOP_K), g, b)


def _route(top_idx, n_exp, bm):
    t = top_idx.shape[0]
    tk = t * TOP_K
    sorted_e, order = lax.sort_key_val(top_idx.reshape(-1), jnp.arange(tk, dtype=jnp.int32))
    experts = jnp.arange(n_exp, dtype=jnp.int32)
    starts = jnp.sum(sorted_e[None, :] < experts[:, None], axis=1, dtype=jnp.int32)
    counts = jnp.sum(sorted_e[None, :] == experts[:, None], axis=1, dtype=jnp.int32)
    padded = ((counts + bm - 1) // bm) * bm
    pends = jnp.cumsum(padded)
    pstarts = pends - padded
    n_blocks = (tk + n_exp * bm) // bm
    blk_start = jnp.arange(n_blocks, dtype=jnp.int32) * bm
    blk_e = jnp.minimum(jnp.sum(pends[None, :] <= blk_start[:, None], axis=1, dtype=jnp.int32),
                        n_exp - 1)
    n_valid = (pends[-1:] // bm).astype(jnp.int32)
    lane = jnp.arange(bm, dtype=jnp.int32)[None, :]
    row_in_expert = (blk_start - pstarts[blk_e])[:, None] + lane
    valid = row_in_expert < counts[blk_e][:, None]
    pick = order[jnp.clip(starts[blk_e][:, None] + row_in_expert, 0, tk - 1)]
    tok, choice = pick // TOP_K, pick % TOP_K
    src = jnp.where(valid, tok, 0)
    spare = tk + blk_start[:, None] + lane - (starts + counts)[blk_e][:, None]
    dst = jnp.where(valid, choice * t + tok, spare)
    return src[:, None, :], dst[:, None, :], blk_e, n_valid


def _channel_dft_table():
    n = np.arange(HEAD)
    ang = 2.0 * np.pi * ((n[:, None] * n[None, :]) % HEAD) / HEAD
    return jnp.asarray(np.concatenate([np.cos(ang), np.sin(ang)], axis=1), dtype=BF16)


def kernel(x_prompt, x_sample, ln_in_g, ln_in_b, w_in, sgu_norm_g, sgu_norm_b, sgu_w, sgu_b,
           w_o, ln1_g, ln1_b, router_w, router_b, w_gu, b_gu, w_down, b_down, ln2_g, ln2_b):
    depth, d, d_in = w_in.shape
    assert depth == 1, "single-layer encoder only"
    n_heads_b, chunk, _ = sgu_w.shape[1:]
    assert chunk == HEAD
    d_b = n_heads_b * HEAD
    d_a = d_in - 2 * d_b
    n_exp = router_w.shape[2]
    d_ff = w_down.shape[2]
    alpha = (2.0 * depth) ** 0.25
    vec = lambda p: p.reshape(1, -1)

    w_in_b = w_in[0].astype(BF16)
    w_o_b = w_o[0].astype(BF16)
    sgu_w_b = sgu_w[0].astype(BF16)
    sgu_bias = jnp.broadcast_to(sgu_b[0][:, :, None], (n_heads_b, HEAD, HEAD))
    cs = _channel_dft_table()
    rw_hi = router_w[0].astype(BF16)
    rw_lo = (router_w[0] - rw_hi.astype(F32)).astype(BF16)
    w_g, w_l = _split_gu(w_gu[0].reshape(n_exp * d, 2 * d_ff))
    w_g = w_g.reshape(n_exp, d, d_ff)
    w_l = w_l.reshape(n_exp, d, d_ff)
    b_g = b_gu[0][:, None, 0::2]
    b_l = b_gu[0][:, None, 1::2]
    w_d = _cast_bf16(w_down[0].reshape(n_exp * d_ff, d)).reshape(n_exp, d_ff, d)
    b_d = b_down[0][:, None, :]

    requests = (x_prompt, x_sample)
    total_rows = sum(x.shape[0] * x.shape[1] for x in requests)
    x1r = jnp.zeros((total_rows * (d // LANES), LANES), F32)
    idx_parts, gate_parts, row0 = [], [], 0
    for x in requests:
        n_batch, seq, _ = x.shape
        x2d = x.reshape(n_batch * seq, d)
        a, q, yb = _mix_in(x2d, vec(ln_in_g), vec(ln_in_b), w_in_b, cs, vec(sgu_norm_g[0]),
                           vec(sgu_norm_b[0]), sgu_w_b, sgu_bias, d_a, d_b)
        ya_lo, ya_hi = _seq_dft(a.reshape(n_batch, seq, d_a), q.reshape(n_batch, seq, d_a))
        x1r, top_idx, top_gates = _mix_out(x2d, ya_lo, ya_hi, yb, vec(ln_in_g), vec(ln_in_b),
                                           w_o_b, vec(ln1_g[0]), vec(ln1_b[0]), rw_hi, rw_lo,
                                           vec(router_b[0]), alpha, row0, x1r)
        idx_parts.append(top_idx)
        gate_parts.append(top_gates)
        row0 += n_batch * seq

    bm = min(MOE_ROWS, total_rows * TOP_K)
    gates = jnp.concatenate(gate_parts, axis=1).T
    src, dst, blk_e, n_valid = _route(jnp.concatenate(idx_parts, axis=1).T, n_exp, bm)
    y_rows = _moe(blk_e, n_valid, src, dst, x1r, w_g, w_l, b_g, b_l, w_d, b_d,
                  total_rows * TOP_K + n_exp * bm)

    outs, row0 = [], 0
    for x in requests:
        rows = x.shape[0] * x.shape[1]
        out = _ln_out(x1r, y_rows, gates, vec(ln2_g[0]), vec(ln2_b[0]), alpha,
                      rows, row0, total_rows)
        outs.append(out.reshape(x.shape))
        row0 += rows
    return tuple(outs)
```

```python
import functools
import math

import jax
import jax.numpy as jnp
import numpy as np
from jax import lax
from jax.experimental import pallas as pl
from jax.experimental.pallas import tpu as pltpu

F32 = jnp.float32
BF16 = jnp.bfloat16

LN_EPS = 1e-5
TOP_K = 4
HEAD = 128
LANES = 128
SWIGLU_LIMIT = 7.0
SWIGLU_ALPHA = 1.702
VMEM_LIMIT_BYTES = 56 * 1024 * 1024

ROW_TILE = 256
MIX_IN_ROWS = 512
DFT_K_TILE = 512
DFT_S_TILE = 1024
DFT_C_TILE = 256
MOE_ROWS = 512
MOE_FF_TILE = 512
MOE_OUT_TILE = 512
MOE_ROW_PAD = 8
PREP_ROWS = 512
SPLIT_GROUP = 512
PREP_STREAMS = 4


def _layer_norm(x, g, b):
    mu = jnp.mean(x, axis=-1, keepdims=True)
    xc = x - mu
    var = jnp.mean(xc * xc, axis=-1, keepdims=True)
    return xc * lax.rsqrt(var + LN_EPS) * g + b


def _gelu(z):
    return 0.5 * z * (1.0 + lax.erf(z * (1.0 / math.sqrt(2.0))))


def _dot(a, b):
    return jnp.dot(a, b, preferred_element_type=F32)


def _params(semantics):
    return pltpu.CompilerParams(dimension_semantics=semantics,
                                vmem_limit_bytes=VMEM_LIMIT_BYTES)


def _whole(shape):
    return pl.BlockSpec(shape, lambda *_: (0,) * len(shape),
                        pipeline_mode=pl.Buffered(1))


def _store_token_rows(ref, val):
    rpt = val.shape[1] // LANES
    for c in range(rpt):
        ref[pl.ds(c, val.shape[0], stride=rpt), :] = val[:, c * LANES:(c + 1) * LANES]


def _load_token_rows(ref, n_tokens, rpt, pitch=None):
    return [ref[pl.ds(c, n_tokens, stride=pitch or rpt), :] for c in range(rpt)]


def _mix_in_body(x_ref, g_ref, b_ref, win_ref, cs_ref, sg_ref, sb_ref, sw_ref, sbias_ref,
                 a_ref, q_ref, yb_ref, *, d_a, d_b):
    xb = _layer_norm(x_ref[...], g_ref[...], b_ref[...]).astype(BF16)
    rows = xb.shape[0]

    za = _dot(xb, win_ref[:, :d_a]).astype(BF16)
    for h in range(d_a // HEAD):
        hs = slice(h * HEAD, (h + 1) * HEAD)
        ab = _dot(za[:, hs], cs_ref[...])
        a_ref[:, hs] = ab[:, :HEAD].astype(BF16)
        q_ref[:, hs] = ab[:, HEAD:].astype(BF16)

    u = _gelu(_dot(xb, win_ref[:, d_a:d_a + d_b]))
    v = _gelu(_dot(xb, win_ref[:, d_a + d_b:]))
    for h in range(d_b // HEAD):
        hs = slice(h * HEAD, (h + 1) * HEAD)
        vn = _layer_norm(v[:, hs], sg_ref[:, hs], sb_ref[:, hs]).astype(BF16)
        for c in range(rows // HEAD):
            rs = slice(c * HEAD, (c + 1) * HEAD)
            gate = _dot(sw_ref[h], vn[rs, :]) + sbias_ref[h]
            yb_ref[rs, hs] = (u[rs, hs] * gate).astype(BF16)


def _mix_in(x2d, ln_g, ln_b, w_in, cs, sgu_g, sgu_b, sgu_w, sgu_bias, d_a, d_b):
    rows, d = x2d.shape
    tm = min(MIX_IN_ROWS, rows)
    n_heads_b = d_b // HEAD
    row = lambda w: pl.BlockSpec((tm, w), lambda i: (i, 0))
    return pl.pallas_call(
        functools.partial(_mix_in_body, d_a=d_a, d_b=d_b),
        grid=(rows // tm,),
        in_specs=[row(d), _whole((1, d)), _whole((1, d)), _whole(w_in.shape),
                  _whole(cs.shape), _whole((1, d_b)), _whole((1, d_b)),
                  _whole((n_heads_b, HEAD, HEAD)), _whole((n_heads_b, HEAD, HEAD))],
        out_specs=[row(d_a), row(d_a), row(d_b)],
        out_shape=[jax.ShapeDtypeStruct((rows, d_a), BF16),
                   jax.ShapeDtypeStruct((rows, d_a), BF16),
                   jax.ShapeDtypeStruct((rows, d_b), BF16)],
        compiler_params=_params(("parallel",)),
        name="mix_in",
    )(x2d, ln_g, ln_b, w_in, cs, sgu_g, sgu_b, sgu_w, sgu_bias)


def _seq_dft_body(a_ref, q_ref, c0_ref, s0_ref, perm_ref, lo_ref, hi_ref,
                  accp_ref, accz_ref, cb_ref, sb_ref, ca_ref, sa_ref, cd_ref, sd_ref,
                  *, seq, scale):
    i, j = pl.program_id(1), pl.program_id(2)
    n_batch, tk, _ = lo_ref.shape
    tkx, ts = cb_ref.shape
    w = 2.0 * math.pi / seq

    @pl.when(j == 0)
    def _():
        accp_ref[...] = jnp.zeros_like(accp_ref)
        accz_ref[...] = jnp.zeros_like(accz_ref)
        ds = lax.broadcasted_iota(jnp.int32, (1, ts), 1)
        ph = (((i * tk) * ds) & (seq - 1)).astype(F32) * w
        ci, si = jnp.cos(ph), jnp.sin(ph)
        cb_ref[...] = ci * c0_ref[...] - si * s0_ref[...]
        sb_ref[...] = si * c0_ref[...] + ci * s0_ref[...]
        k = i * tk + lax.broadcasted_iota(jnp.int32, (tkx, HEAD), 0)
        pd = ((k * ts) & (seq - 1)).astype(F32) * w
        cd_ref[...] = jnp.cos(pd)
        sd_ref[...] = jnp.sin(pd)
        ca_ref[...] = jnp.ones_like(ca_ref)
        sa_ref[...] = jnp.zeros_like(sa_ref)

    ca, sa = ca_ref[...], sa_ref[...]
    cos_parts, sin_parts = [], []
    for g in range(ts // HEAD):
        gs = slice(g * HEAD, (g + 1) * HEAD)
        cb, sb = cb_ref[:, gs], sb_ref[:, gs]
        cos_parts.append((ca * cb - sa * sb).astype(BF16))
        sin_parts.append((sa * cb + ca * sb).astype(BF16))
    cos_t = jnp.concatenate(cos_parts, axis=1)
    sin_t = jnp.concatenate(sin_parts, axis=1)
    for b in range(n_batch):
        accp_ref[b] += _dot(cos_t, a_ref[b])
        accz_ref[b] += _dot(sin_t, q_ref[b])
    cd, sd = cd_ref[...], sd_ref[...]
    ca_ref[...] = ca * cd - sa * sd
    sa_ref[...] = sa * cd + ca * sd

    @pl.when(j == pl.num_programs(2) - 1)
    def _():
        first_row = lax.broadcasted_iota(jnp.int32, (tk, 1), 0) == 0
        for b in range(n_batch):
            p, z = accp_ref[b], accz_ref[b]
            lo_ref[b] = ((p[:tk] - z[:tk]) * scale).astype(BF16)
            mirror = ((p + z) * scale).astype(BF16)
            rev = _dot(perm_ref[...], mirror[:tk]).astype(BF16)
            hi_ref[b] = jnp.where(first_row, mirror[tk:tk + 1], rev)


def _seq_dft(a3d, q3d):
    n_batch, seq, d_a = a3d.shape
    assert seq & (seq - 1) == 0, "sequence length must be a power of two"
    half = seq // 2
    tk, ts, tc = min(DFT_K_TILE, half), min(DFT_S_TILE, seq), min(DFT_C_TILE, d_a)
    tkx = tk + 8
    n_i = half // tk
    scale = 1.0 / math.sqrt(seq * HEAD)
    ang = 2.0 * np.pi * np.outer(np.arange(tkx), np.arange(ts)) / seq
    c0 = jnp.asarray(np.cos(ang), dtype=F32)
    s0 = jnp.asarray(np.sin(ang), dtype=F32)
    perm = np.zeros((tk, tk), np.float32)
    r = np.arange(1, tk)
    perm[r, tk - r] = 1.0
    in_spec = pl.BlockSpec((n_batch, ts, tc), lambda c, i, j: (0, j, c))
    out = jax.ShapeDtypeStruct((n_batch, half, d_a), BF16)
    return pl.pallas_call(
        functools.partial(_seq_dft_body, seq=seq, scale=scale),
        grid=(d_a // tc, n_i, seq // ts),
        in_specs=[in_spec, in_spec, _whole((tkx, ts)), _whole((tkx, ts)), _whole((tk, tk))],
        out_specs=[pl.BlockSpec((n_batch, tk, tc), lambda c, i, j: (0, i, c)),
                   pl.BlockSpec((n_batch, tk, tc), lambda c, i, j: (0, n_i - 1 - i, c))],
        out_shape=[out, out],
        scratch_shapes=[pltpu.VMEM((n_batch, tkx, tc), F32), pltpu.VMEM((n_batch, tkx, tc), F32),
                        pltpu.VMEM((tkx, ts), F32), pltpu.VMEM((tkx, ts), F32)]
                       + [pltpu.VMEM((tkx, HEAD), F32)] * 4,
        compiler_params=_params(("parallel", "parallel", "arbitrary")),
        name="seq_dft",
    )(a3d, q3d, c0, s0, jnp.asarray(perm, dtype=BF16))


def _top_k_gates(logits_t):
    n_exp, n_tok = logits_t.shape
    expert = lax.broadcasted_iota(jnp.int32, (n_exp, n_tok), 0)
    slot = lax.broadcasted_iota(jnp.int32, (TOP_K, n_tok), 0)
    work = logits_t
    top = jnp.zeros((TOP_K, n_tok), F32)
    idx = jnp.zeros((TOP_K, n_tok), jnp.int32)
    for c in range(TOP_K):
        best = jnp.max(work, axis=0, keepdims=True)
        where = jnp.min(jnp.where(work == best, expert, n_exp), axis=0, keepdims=True)
        top = jnp.where(slot == c, best, top)
        idx = jnp.where(slot == c, where, idx)
        work = jnp.where(expert == where, -jnp.inf, work)
    e = jnp.exp(top - jnp.max(top, axis=0, keepdims=True))
    return idx, e / jnp.sum(e, axis=0, keepdims=True)


def _mix_out_body(x_ref, lo_ref, hi_ref, yb_ref, g0_ref, b0_ref, wo_ref, g1_ref, b1_ref,
                  rwh_ref, rwl_ref, rb_ref, x1r_in_ref, x1r_ref, idx_ref, gate_ref, *, alpha, d_a):
    del x1r_in_ref
    xln = _layer_norm(x_ref[...], g0_ref[...], b0_ref[...])
    in_low_half = pl.program_id(1) < pl.num_programs(1) // 2
    ya = jnp.where(in_low_half, lo_ref[...], hi_ref[...])
    mix = _dot(ya, wo_ref[:d_a, :]) + _dot(yb_ref[...], wo_ref[d_a:, :])
    x1 = _layer_norm(alpha * xln + mix, g1_ref[...], b1_ref[...])
    _store_token_rows(x1r_ref, x1)
    hi = x1.astype(BF16)
    lo = (x1 - hi.astype(F32)).astype(BF16)
    logits = (_dot(hi, rwh_ref[...]) + _dot(hi, rwl_ref[...])
              + _dot(lo, rwh_ref[...]) + rb_ref[...])
    idx_ref[...], gate_ref[...] = _top_k_gates(logits.T)


def _mix_out(x2d, ya_lo, ya_hi, yb, g0, b0, w_o, g1, b1, rw_hi, rw_lo, rb, alpha, row0, x1r_all):
    rows, d = x2d.shape
    n_batch, half, d_a = ya_lo.shape
    n_exp = rw_hi.shape[1]
    tm = min(ROW_TILE, half)
    n_p = 2 * half // tm
    col = pl.BlockSpec((TOP_K, tm), lambda b, p: (0, b * n_p + p))
    rpt = d // LANES
    assert row0 % tm == 0
    row = lambda w: pl.BlockSpec((tm, w), lambda b, p: (b * n_p + p, 0))
    vec = _whole((1, d))
    return pl.pallas_call(
        functools.partial(_mix_out_body, alpha=alpha, d_a=d_a),
        grid=(n_batch, n_p),
        in_specs=[row(d),
                  pl.BlockSpec((None, tm, d_a), lambda b, p: (b, jnp.minimum(p, n_p // 2 - 1), 0)),
                  pl.BlockSpec((None, tm, d_a), lambda b, p: (b, jnp.maximum(p - n_p // 2, 0), 0)),
                  row(yb.shape[1]), vec, vec, _whole(w_o.shape), vec, vec,
                  _whole(rw_hi.shape), _whole(rw_lo.shape), _whole((1, n_exp)),
                  pl.BlockSpec(memory_space=pl.ANY)],
        out_specs=[pl.BlockSpec((tm * rpt, LANES), lambda b, p: (row0 // tm + b * n_p + p, 0)),
                   col, col],
        out_shape=[jax.ShapeDtypeStruct(x1r_all.shape, F32),
                   jax.ShapeDtypeStruct((TOP_K, rows), jnp.int32),
                   jax.ShapeDtypeStruct((TOP_K, rows), F32)],
        input_output_aliases={12: 0},
        compiler_params=_params(("parallel", "parallel")),
        name="mix_out",
    )(x2d, ya_lo, ya_hi, yb, g0, b0, w_o, g1, b1, rw_hi, rw_lo, rb, x1r_all)


def _column_streams(w2d, tr):
    rows, cols = w2d.shape
    n = PREP_STREAMS if cols % (PREP_STREAMS * SPLIT_GROUP) == 0 else 1
    specs = [pl.BlockSpec((tr, cols // n), functools.partial(lambda i, q: (i, q), q=q))
             for q in range(n)]
    return [w2d] * n, specs


def _split_gu_body(*refs):
    *w_refs, perm_ref, g_ref, l_ref = refs
    half = SPLIT_GROUP // 2
    groups = w_refs[0].shape[1] // SPLIT_GROUP
    for q, w_ref in enumerate(w_refs):
        for c in range(groups):
            w = w_ref[:, c * SPLIT_GROUP:(c + 1) * SPLIT_GROUP].astype(BF16)
            r = _dot(w, perm_ref[...])
            o = (q * groups + c) * half
            g_ref[:, o:o + half] = r[:, :half].astype(BF16)
            l_ref[:, o:o + half] = r[:, half:].astype(BF16)


def _split_gu(w2d):
    rows, cols = w2d.shape
    assert cols % SPLIT_GROUP == 0
    tr = min(PREP_ROWS, rows)
    perm = np.zeros((SPLIT_GROUP, SPLIT_GROUP), np.float32)
    j = np.arange(SPLIT_GROUP // 2)
    perm[2 * j, j] = 1.0
    perm[2 * j + 1, SPLIT_GROUP // 2 + j] = 1.0
    streams, specs = _column_streams(w2d, tr)
    out = jax.ShapeDtypeStruct((rows, cols // 2), BF16)
    return pl.pallas_call(
        _split_gu_body,
        grid=(rows // tr,),
        in_specs=specs + [_whole(perm.shape)],
        out_specs=[pl.BlockSpec((tr, cols // 2), lambda i: (i, 0))] * 2,
        out_shape=[out, out],
        compiler_params=_params(("parallel",)),
        name="split_gu",
    )(*streams, jnp.asarray(perm, dtype=BF16))


def _cast_body(*refs):
    *w_refs, o_ref = refs
    width = w_refs[0].shape[1]
    for q, w_ref in enumerate(w_refs):
        o_ref[:, q * width:(q + 1) * width] = w_ref[...].astype(BF16)


def _cast_bf16(w2d):
    rows, cols = w2d.shape
    tr = min(PREP_ROWS, rows)
    streams, specs = _column_streams(w2d, tr)
    return pl.pallas_call(
        _cast_body,
        grid=(rows // tr,),
        in_specs=specs,
        out_specs=pl.BlockSpec((tr, cols), lambda i: (i, 0)),
        out_shape=jax.ShapeDtypeStruct((rows, cols), BF16),
        compiler_params=_params(("parallel",)),
        name="cast_bf16",
    )(*streams)


def _moe_body(be_ref, nv_ref, src_ref, src_next_ref, dst_ref, dst_prev_ref, x_hbm, wg_hbm,
              wl_hbm, bg_ref, bl_ref, wd_hbm, bd_ref, y_hbm, xbuf_ref, xs_ref, act_ref, obuf_ref,
              wg_ref, wl_ref, wd_ref, gsem, ssem, wsem):
    i, n_valid = pl.program_id(0), nv_ref[0]
    e = be_ref[i]
    e_next = be_ref[jnp.minimum(i + 1, pl.num_programs(0) - 1)]
    first_of_expert = jnp.logical_or(i == 0, be_ref[jnp.maximum(i - 1, 0)] != e)
    last_of_expert = jnp.logical_and(i + 1 < n_valid, e_next != e)

    def up_copies(expert):
        return (pltpu.make_async_copy(wg_hbm.at[expert], wg_ref, wsem.at[0]),
                pltpu.make_async_copy(wl_hbm.at[expert], wl_ref, wsem.at[1]))

    def down_copy(expert):
        return pltpu.make_async_copy(wd_hbm.at[expert], wd_ref, wsem.at[2])

    bm, d = xs_ref.shape
    rpt = d // LANES
    pitch = obuf_ref.shape[0] // bm
    slot = i % 2

    def gather_copy(idx_ref, r, to_slot):
        tok = idx_ref[0, 0, r]
        return pltpu.make_async_copy(x_hbm.at[pl.ds(pl.multiple_of(tok * rpt, rpt), rpt), :],
                                     xbuf_ref.at[to_slot, pl.ds(r * pitch, rpt), :], gsem)

    def scatter_copy(idx_ref, r):
        row = idx_ref[0, 0, r]
        return pltpu.make_async_copy(obuf_ref.at[pl.ds(r * pitch, rpt), :],
                                     y_hbm.at[pl.ds(pl.multiple_of(row * rpt, rpt), rpt), :], ssem)

    def wait_gather(of_slot):
        pltpu.make_async_copy(x_hbm.at[pl.ds(0, bm * rpt), :],
                              xbuf_ref.at[of_slot, pl.ds(0, bm * rpt), :], gsem).wait()

    def wait_scatter():
        pltpu.make_async_copy(obuf_ref.at[pl.ds(0, bm * rpt), :],
                              y_hbm.at[pl.ds(0, bm * rpt), :], ssem).wait()

    @pl.when(i < n_valid)
    def _():
        @pl.when(i == 0)
        def _():
            obuf_ref[...] = jnp.zeros_like(obuf_ref)
            for r in range(bm):
                gather_copy(src_ref, r, 0).start(priority=r % 2)
            for copy in up_copies(e):
                copy.start()

        @pl.when(first_of_expert)
        def _():
            down_copy(e).start()
            for copy in up_copies(e):
                copy.wait()

        wait_gather(slot)
        for c, part in enumerate(_load_token_rows(xbuf_ref.at[slot], bm, rpt, pitch)):
            xs_ref[:, c * LANES:(c + 1) * LANES] = part.astype(BF16)
        x = xs_ref[...]
        d_ff = wg_ref.shape[1]
        tf = min(MOE_FF_TILE, d_ff)
        n_ff = d_ff // tf
        for f in range(n_ff):
            fs = slice(f * tf, (f + 1) * tf)
            glu = jnp.minimum(_dot(x, wg_ref[:, fs]) + bg_ref[0, :, fs], SWIGLU_LIMIT)
            lin = jnp.clip(_dot(x, wl_ref[:, fs]) + bl_ref[0, :, fs],
                           -SWIGLU_LIMIT, SWIGLU_LIMIT)
            for r in range(f * bm // n_ff, (f + 1) * bm // n_ff):
                gather_copy(src_next_ref, r, 1 - slot).start(priority=r % 2)
                scatter_copy(dst_prev_ref, r).start(priority=r % 2)
            act = glu * (1.0 / (1.0 + jnp.exp(-SWIGLU_ALPHA * glu))) * (lin + 1.0)
            act_ref[:, fs] = act.astype(BF16)

        wait_scatter()

        @pl.when(last_of_expert)
        def _():
            for copy in up_copies(e_next):
                copy.start()

        @pl.when(first_of_expert)
        def _():
            down_copy(e).wait()

        tn = min(MOE_OUT_TILE, d)
        for n in range(d // tn):
            ns = slice(n * tn, (n + 1) * tn)
            done = _dot(act_ref[...], wd_ref[:, ns]) + bd_ref[0, :, ns]
            for c in range(tn // LANES):
                obuf_ref[pl.ds(n * (tn // LANES) + c, bm, stride=pitch), :] = (
                    done[:, c * LANES:(c + 1) * LANES])

        @pl.when(i == n_valid - 1)
        def _():
            for r in range(bm):
                scatter_copy(dst_ref, r).start(priority=r % 2)
            wait_gather(1 - slot)
            wait_scatter()


def _moe(blk_e, n_valid, src, dst, x1r, w_g, w_l, b_g, b_l, w_d, b_d, n_out_rows):
    n_blocks, _, bm = src.shape
    n_exp, d, d_ff = w_g.shape
    rpt = d // LANES
    pitch = rpt + MOE_ROW_PAD if rpt % 16 == 0 else rpt
    expert = lambda i, be, nv: (be[i], 0, 0)
    idx_block = lambda index_map: pl.BlockSpec((1, 1, bm), index_map, memory_space=pltpu.SMEM)
    grid_spec = pltpu.PrefetchScalarGridSpec(
        num_scalar_prefetch=2,
        grid=(n_blocks,),
        in_specs=[
            idx_block(lambda i, be, nv: (i, 0, 0)),
            idx_block(lambda i, be, nv: (jnp.minimum(i + 1, n_blocks - 1), 0, 0)),
            idx_block(lambda i, be, nv: (i, 0, 0)),
            idx_block(lambda i, be, nv: (jnp.maximum(i - 1, 0), 0, 0)),
            pl.BlockSpec(memory_space=pl.ANY),
            pl.BlockSpec(memory_space=pl.ANY),
            pl.BlockSpec(memory_space=pl.ANY),
            pl.BlockSpec((1, 1, d_ff), expert),
            pl.BlockSpec((1, 1, d_ff), expert),
            pl.BlockSpec(memory_space=pl.ANY),
            pl.BlockSpec((1, 1, d), expert),
        ],
        out_specs=pl.BlockSpec(memory_space=pl.ANY),
        scratch_shapes=[pltpu.VMEM((2, bm * pitch, LANES), F32), pltpu.VMEM((bm, d), BF16),
                        pltpu.VMEM((bm, d_ff), BF16), pltpu.VMEM((bm * pitch, LANES), F32),
                        pltpu.VMEM((d, d_ff), BF16), pltpu.VMEM((d, d_ff), BF16),
                        pltpu.VMEM((d_ff, d), BF16),
                        pltpu.SemaphoreType.DMA(()), pltpu.SemaphoreType.DMA(()),
                        pltpu.SemaphoreType.DMA((3,))],
    )
    return pl.pallas_call(
        _moe_body,
        grid_spec=grid_spec,
        out_shape=jax.ShapeDtypeStruct((n_out_rows * rpt, LANES), F32),
        compiler_params=_params(("arbitrary",)),
        name="moe",
    )(blk_e, n_valid, src, src, dst, dst, x1r, w_g, w_l, b_g, b_l, w_d, b_d)


def _ln_out_body(x1r_ref, gate_ref, *refs, alpha):
    *y_refs, g_ref, b_ref, o_ref = refs
    tm, d = o_ref.shape
    rpt = d // LANES
    gates = gate_ref[...]
    parts = [alpha * p for p in _load_token_rows(x1r_ref, tm, rpt)]
    for c, y_ref in enumerate(y_refs):
        gate = jnp.broadcast_to(gates[:, c:c + 1], (tm, LANES))
        parts = [p + gate * y for p, y in zip(parts, _load_token_rows(y_ref, tm, rpt))]
    o_ref[...] = _layer_norm(jnp.concatenate(parts, axis=1), g_ref[...], b_ref[...])


def _ln_out(x1r, y_rows, gates, g, b, alpha, rows, row0, total_rows):
    d = g.shape[1]
    rpt = d // LANES
    tm = min(ROW_TILE, rows)
    assert row0 % tm == 0 and total_rows % tm == 0
    tile0 = row0 // tm
    tok = lambda off: pl.BlockSpec((tm * rpt, LANES), lambda i: (off + tile0 + i, 0))
    return pl.pallas_call(
        functools.partial(_ln_out_body, alpha=alpha),
        grid=(rows // tm,),
        in_specs=[tok(0), pl.BlockSpec((tm, TOP_K), lambda i: (tile0 + i, 0))]
                 + [tok(c * (total_rows // tm)) for c in range(TOP_K)]
                 + [_whole((1, d)), _whole((1, d))],
        out_specs=pl.BlockSpec((tm, d), lambda i: (i, 0)),
        out_shape=jax.ShapeDtypeStruct((rows, d), F32),
        compiler_params=_params(("parallel",)),
        name="ln_out",
    )(x1r, gates, *([y_rows] * TOP_K), g, b)


def _route(top_idx, n_exp, bm):
    t = top_idx.shape[0]
    tk = t * TOP_K
    sorted_e, order = lax.sort_key_val(top_idx.reshape(-1), jnp.arange(tk, dtype=jnp.int32))
    experts = jnp.arange(n_exp, dtype=jnp.int32)
    starts = jnp.sum(sorted_e[None, :] < experts[:, None], axis=1, dtype=jnp.int32)
    counts = jnp.sum(sorted_e[None, :] == experts[:, None], axis=1, dtype=jnp.int32)
    padded = ((counts + bm - 1) // bm) * bm
    pends = jnp.cumsum(padded)
    pstarts = pends - padded
    n_blocks = (tk + n_exp * bm) // bm
    blk_start = jnp.arange(n_blocks, dtype=jnp.int32) * bm
    blk_e = jnp.minimum(jnp.sum(pends[None, :] <= blk_start[:, None], axis=1, dtype=jnp.int32),
                        n_exp - 1)
    n_valid = (pends[-1:] // bm).astype(jnp.int32)
    lane = jnp.arange(bm, dtype=jnp.int32)[None, :]
    row_in_expert = (blk_start - pstarts[blk_e])[:, None] + lane
    valid = row_in_expert < counts[blk_e][:, None]
    pick = order[jnp.clip(starts[blk_e][:, None] + row_in_expert, 0, tk - 1)]
    tok, choice = pick // TOP_K, pick % TOP_K
    src = jnp.where(valid, tok, 0)
    spare = tk + blk_start[:, None] + lane - (starts + counts)[blk_e][:, None]
    dst = jnp.where(valid, choice * t + tok, spare)
    return src[:, None, :], dst[:, None, :], blk_e, n_valid


def _channel_dft_table():
    n = np.arange(HEAD)
    ang = 2.0 * np.pi * ((n[:, None] * n[None, :]) % HEAD) / HEAD
    return jnp.asarray(np.concatenate([np.cos(ang), np.sin(ang)], axis=1), dtype=BF16)


def kernel(x_prompt, x_sample, ln_in_g, ln_in_b, w_in, sgu_norm_g, sgu_norm_b, sgu_w, sgu_b,
           w_o, ln1_g, ln1_b, router_w, router_b, w_gu, b_gu, w_down, b_down, ln2_g, ln2_b):
    depth, d, d_in = w_in.shape
    assert depth == 1, "single-layer encoder only"
    n_heads_b, chunk, _ = sgu_w.shape[1:]
    assert chunk == HEAD
    d_b = n_heads_b * HEAD
    d_a = d_in - 2 * d_b
    n_exp = router_w.shape[2]
    d_ff = w_down.shape[2]
    alpha = (2.0 * depth) ** 0.25
    vec = lambda p: p.reshape(1, -1)

    w_in_b = w_in[0].astype(BF16)
    w_o_b = w_o[0].astype(BF16)
    sgu_w_b = sgu_w[0].astype(BF16)
    sgu_bias = jnp.broadcast_to(sgu_b[0][:, :, None], (n_heads_b, HEAD, HEAD))
    cs = _channel_dft_table()
    rw_hi = router_w[0].astype(BF16)
    rw_lo = (router_w[0] - rw_hi.astype(F32)).astype(BF16)
    w_g, w_l = _split_gu(w_gu[0].reshape(n_exp * d, 2 * d_ff))
    w_g = w_g.reshape(n_exp, d, d_ff)
    w_l = w_l.reshape(n_exp, d, d_ff)
    b_g = b_gu[0][:, None, 0::2]
    b_l = b_gu[0][:, None, 1::2]
    w_d = _cast_bf16(w_down[0].reshape(n_exp * d_ff, d)).reshape(n_exp, d_ff, d)
    b_d = b_down[0][:, None, :]

    requests = (x_prompt, x_sample)
    total_rows = sum(x.shape[0] * x.shape[1] for x in requests)
    x1r = jnp.zeros((total_rows * (d // LANES), LANES), F32)
    idx_parts, gate_parts, row0 = [], [], 0
    for x in requests:
        n_batch, seq, _ = x.shape
        x2d = x.reshape(n_batch * seq, d)
        a, q, yb = _mix_in(x2d, vec(ln_in_g), vec(ln_in_b), w_in_b, cs, vec(sgu_norm_g[0]),
                           vec(sgu_norm_b[0]), sgu_w_b, sgu_bias, d_a, d_b)
        ya_lo, ya_hi = _seq_dft(a.reshape(n_batch, seq, d_a), q.reshape(n_batch, seq, d_a))
        x1r, top_idx, top_gates = _mix_out(x2d, ya_lo, ya_hi, yb, vec(ln_in_g), vec(ln_in_b),
                                           w_o_b, vec(ln1_g[0]), vec(ln1_b[0]), rw_hi, rw_lo,
                                           vec(router_b[0]), alpha, row0, x1r)
        idx_parts.append(top_idx)
        gate_parts.append(top_gates)
        row0 += n_batch * seq

    bm = min(MOE_ROWS, total_rows * TOP_K)
    gates = jnp.concatenate(gate_parts, axis=1).T
    src, dst, blk_e, n_valid = _route(jnp.concatenate(idx_parts, axis=1).T, n_exp, bm)
    y_rows = _moe(blk_e, n_valid, src, dst, x1r, w_g, w_l, b_g, b_l, w_d, b_d,
                  total_rows * TOP_K + n_exp * bm)

    outs, row0 = [], 0
    for x in requests:
        rows = x.shape[0] * x.shape[1]
        out = _ln_out(x1r, y_rows, gates, vec(ln2_g[0]), vec(ln2_b[0]), alpha,
                      rows, row0, total_rows)
        outs.append(out.reshape(x.shape))
        row0 += rows
    return tuple(outs)
```

```python
import functools
import math

import jax
import jax.numpy as jnp
import numpy as np
from jax import lax
from jax.experimental import pallas as pl
from jax.experimental.pallas import tpu as pltpu

F32 = jnp.float32
BF16 = jnp.bfloat16

LN_EPS = 1e-5
TOP_K = 4
HEAD = 128
LANES = 128
SWIGLU_LIMIT = 7.0
SWIGLU_ALPHA = 1.702
VMEM_LIMIT_BYTES = 56 * 1024 * 1024

ROW_TILE = 256
MIX_IN_ROWS = 512
DFT_K_TILE = 512
DFT_S_TILE = 1024
DFT_C_TILE = 256
MOE_ROWS = 512
MOE_FF_TILE = 512
MOE_OUT_TILE = 512
MOE_ROW_PAD = 8
PREP_ROWS = 512
SPLIT_GROUP = 512
PREP_STREAMS = 4


def _layer_norm(x, g, b):
    mu = jnp.mean(x, axis=-1, keepdims=True)
    xc = x - mu
    var = jnp.mean(xc * xc, axis=-1, keepdims=True)
    return xc * lax.rsqrt(var + LN_EPS) * g + b


def _gelu(z):
    return 0.5 * z * (1.0 + lax.erf(z * (1.0 / math.sqrt(2.0))))


def _dot(a, b):
    return jnp.dot(a, b, preferred_element_type=F32)


def _params(semantics):
    return pltpu.CompilerParams(dimension_semantics=semantics,
                                vmem_limit_bytes=VMEM_LIMIT_BYTES)


def _whole(shape):
    return pl.BlockSpec(shape, lambda *_: (0,) * len(shape),
                        pipeline_mode=pl.Buffered(1))


def _store_token_rows(ref, val):
    rpt = val.shape[1] // LANES
    for c in range(rpt):
        ref[pl.ds(c, val.shape[0], stride=rpt), :] = val[:, c * LANES:(c + 1) * LANES]


def _load_token_rows(ref, n_tokens, rpt, pitch=None):
    return [ref[pl.ds(c, n_tokens, stride=pitch or rpt), :] for c in range(rpt)]


def _mix_in_body(x_ref, g_ref, b_ref, win_ref, cs_ref, sg_ref, sb_ref, sw_ref, sbias_ref,
                 a_ref, q_ref, yb_ref, *, d_a, d_b):
    xb = _layer_norm(x_ref[...], g_ref[...], b_ref[...]).astype(BF16)
    rows = xb.shape[0]

    za = _dot(xb, win_ref[:, :d_a]).astype(BF16)
    for h in range(d_a // HEAD):
        hs = slice(h * HEAD, (h + 1) * HEAD)
        ab = _dot(za[:, hs], cs_ref[...])
        a_ref[:, hs] = ab[:, :HEAD].astype(BF16)
        q_ref[:, hs] = ab[:, HEAD:].astype(BF16)

    u = _gelu(_dot(xb, win_ref[:, d_a:d_a + d_b]))
    v = _gelu(_dot(xb, win_ref[:, d_a + d_b:]))
    for h in range(d_b // HEAD):
        hs = slice(h * HEAD, (h + 1) * HEAD)
        vn = _layer_norm(v[:, hs], sg_ref[:, hs], sb_ref[:, hs]).astype(BF16)
        for c in range(rows // HEAD):
            rs = slice(c * HEAD, (c + 1) * HEAD)
            gate = _dot(sw_ref[h], vn[rs, :]) + sbias_ref[h]
            yb_ref[rs, hs] = (u[rs, hs] * gate).astype(BF16)


def _mix_in(x2d, ln_g, ln_b, w_in, cs, sgu_g, sgu_b, sgu_w, sgu_bias, d_a, d_b):
    rows, d = x2d.shape
    tm = min(MIX_IN_ROWS, rows)
    n_heads_b = d_b // HEAD
    row = lambda w: pl.BlockSpec((tm, w), lambda i: (i, 0))
    return pl.pallas_call(
        functools.partial(_mix_in_body, d_a=d_a, d_b=d_b),
        grid=(rows // tm,),
        in_specs=[row(d), _whole((1, d)), _whole((1, d)), _whole(w_in.shape),
                  _whole(cs.shape), _whole((1, d_b)), _whole((1, d_b)),
                  _whole((n_heads_b, HEAD, HEAD)), _whole((n_heads_b, HEAD, HEAD))],
        out_specs=[row(d_a), row(d_a), row(d_b)],
        out_shape=[jax.ShapeDtypeStruct((rows, d_a), BF16),
                   jax.ShapeDtypeStruct((rows, d_a), BF16),
                   jax.ShapeDtypeStruct((rows, d_b), BF16)],
        compiler_params=_params(("parallel",)),
        name="mix_in",
    )(x2d, ln_g, ln_b, w_in, cs, sgu_g, sgu_b, sgu_w, sgu_bias)


def _seq_dft_body(a_ref, q_ref, c0_ref, s0_ref, perm_ref, lo_ref, hi_ref,
                  accp_ref, accz_ref, cb_ref, sb_ref, ca_ref, sa_ref, cd_ref, sd_ref,
                  *, seq, scale):
    i, j = pl.program_id(1), pl.program_id(2)
    n_batch, tk, _ = lo_ref.shape
    tkx, ts = cb_ref.shape
    w = 2.0 * math.pi / seq

    @pl.when(j == 0)
    def _():
        accp_ref[...] = jnp.zeros_like(accp_ref)
        accz_ref[...] = jnp.zeros_like(accz_ref)
        ds = lax.broadcasted_iota(jnp.int32, (1, ts), 1)
        ph = (((i * tk) * ds) & (seq - 1)).astype(F32) * w
        ci, si = jnp.cos(ph), jnp.sin(ph)
        cb_ref[...] = ci * c0_ref[...] - si * s0_ref[...]
        sb_ref[...] = si * c0_ref[...] + ci * s0_ref[...]
        k = i * tk + lax.broadcasted_iota(jnp.int32, (tkx, HEAD), 0)
        pd = ((k * ts) & (seq - 1)).astype(F32) * w
        cd_ref[...] = jnp.cos(pd)
        sd_ref[...] = jnp.sin(pd)
        ca_ref[...] = jnp.ones_like(ca_ref)
        sa_ref[...] = jnp.zeros_like(sa_ref)

    ca, sa = ca_ref[...], sa_ref[...]
    cos_parts, sin_parts = [], []
    for g in range(ts // HEAD):
        gs = slice(g * HEAD, (g + 1) * HEAD)
        cb, sb = cb_ref[:, gs], sb_ref[:, gs]
        cos_parts.append((ca * cb - sa * sb).astype(BF16))
        sin_parts.append((sa * cb + ca * sb).astype(BF16))
    cos_t = jnp.concatenate(cos_parts, axis=1)
    sin_t = jnp.concatenate(sin_parts, axis=1)
    for b in range(n_batch):
        accp_ref[b] += _dot(cos_t, a_ref[b])
        accz_ref[b] += _dot(sin_t, q_ref[b])
    cd, sd = cd_ref[...], sd_ref[...]
    ca_ref[...] = ca * cd - sa * sd
    sa_ref[...] = sa * cd + ca * sd

    @pl.when(j == pl.num_programs(2) - 1)
    def _():
        first_row = lax.broadcasted_iota(jnp.int32, (tk, 1), 0) == 0
        for b in range(n_batch):
            p, z = accp_ref[b], accz_ref[b]
            lo_ref[b] = ((p[:tk] - z[:tk]) * scale).astype(BF16)
            mirror = ((p + z) * scale).astype(BF16)
            rev = _dot(perm_ref[...], mirror[:tk]).astype(BF16)
            hi_ref[b] = jnp.where(first_row, mirror[tk:tk + 1], rev)


def _seq_dft(a3d, q3d):
    n_batch, seq, d_a = a3d.shape
    assert seq & (seq - 1) == 0, "sequence length must be a power of two"
    half = seq // 2
    tk, ts, tc = min(DFT_K_TILE, half), min(DFT_S_TILE, seq), min(DFT_C_TILE, d_a)
    tkx = tk + 8
    n_i = half // tk
    scale = 1.0 / math.sqrt(seq * HEAD)
    ang = 2.0 * np.pi * np.outer(np.arange(tkx), np.arange(ts)) / seq
    c0 = jnp.asarray(np.cos(ang), dtype=F32)
    s0 = jnp.asarray(np.sin(ang), dtype=F32)
    perm = np.zeros((tk, tk), np.float32)
    r = np.arange(1, tk)
    perm[r, tk - r] = 1.0
    in_spec = pl.BlockSpec((n_batch, ts, tc), lambda c, i, j: (0, j, c))
    out = jax.ShapeDtypeStruct((n_batch, half, d_a), BF16)
    return pl.pallas_call(
        functools.partial(_seq_dft_body, seq=seq, scale=scale),
        grid=(d_a // tc, n_i, seq // ts),
        in_specs=[in_spec, in_spec, _whole((tkx, ts)), _whole((tkx, ts)), _whole((tk, tk))],
        out_specs=[pl.BlockSpec((n_batch, tk, tc), lambda c, i, j: (0, i, c)),
                   pl.BlockSpec((n_batch, tk, tc), lambda c, i, j: (0, n_i - 1 - i, c))],
        out_shape=[out, out],
        scratch_shapes=[pltpu.VMEM((n_batch, tkx, tc), F32), pltpu.VMEM((n_batch, tkx, tc), F32),
                        pltpu.VMEM((tkx, ts), F32), pltpu.VMEM((tkx, ts), F32)]
                       + [pltpu.VMEM((tkx, HEAD), F32)] * 4,
        compiler_params=_params(("parallel", "parallel", "arbitrary")),
        name="seq_dft",
    )(a3d, q3d, c0, s0, jnp.asarray(perm, dtype=BF16))


def _top_k_gates(logits_t):
    n_exp, n_tok = logits_t.shape
    expert = lax.broadcasted_iota(jnp.int32, (n_exp, n_tok), 0)
    slot = lax.broadcasted_iota(jnp.int32, (TOP_K, n_tok), 0)
    work = logits_t
    top = jnp.zeros((TOP_K, n_tok), F32)
    idx = jnp.zeros((TOP_K, n_tok), jnp.int32)
    for c in range(TOP_K):
        best = jnp.max(work, axis=0, keepdims=True)
        where = jnp.min(jnp.where(work == best, expert, n_exp), axis=0, keepdims=True)
        top = jnp.where(slot == c, best, top)
        idx = jnp.where(slot == c, where, idx)
        work = jnp.where(expert == where, -jnp.inf, work)
    e = jnp.exp(top - jnp.max(top, axis=0, keepdims=True))
    return idx, e / jnp.sum(e, axis=0, keepdims=True)


def _mix_out_body(*refs, alpha, d_a, seq_tiles):
    n_req = len(seq_tiles)
    per_req = [refs[4 * q:4 * q + 4] for q in range(n_req)]
    (g0_ref, b0_ref, wo_ref, g1_ref, b1_ref, rwh_ref, rwl_ref, rb_ref,
     x1r_ref, idx_ref, gate_ref) = refs[4 * n_req:]
    req, tile = pl.program_id(0), pl.program_id(1)

    def active(k):
        val = per_req[0][k][...]
        for q in range(1, n_req):
            val = jnp.where(req == q, per_req[q][k][...], val)
        return val

    n_p = seq_tiles[0]
    for q in range(1, n_req):
        n_p = jnp.where(req == q, seq_tiles[q], n_p)
    in_low_half = tile % n_p < n_p // 2
    xln = _layer_norm(active(0), g0_ref[...], b0_ref[...])
    ya = jnp.where(in_low_half, active(1), active(2))
    mix = _dot(ya, wo_ref[:d_a, :]) + _dot(active(3), wo_ref[d_a:, :])
    x1 = _layer_norm(alpha * xln + mix, g1_ref[...], b1_ref[...])
    _store_token_rows(x1r_ref, x1)
    hi = x1.astype(BF16)
    lo = (x1 - hi.astype(F32)).astype(BF16)
    logits = (_dot(hi, rwh_ref[...]) + _dot(hi, rwl_ref[...])
              + _dot(lo, rwh_ref[...]) + rb_ref[...])
    idx_ref[...], gate_ref[...] = _top_k_gates(logits.T)


def _mix_out(xs, ya_los, ya_his, ybs, g0, b0, w_o, g1, b1, rw_hi, rw_lo, rb, alpha):
    n_req = len(xs)
    rows, d = xs[0].shape
    assert all(x.shape == (rows, d) for x in xs)
    d_a = ya_los[0].shape[2]
    n_exp = rw_hi.shape[1]
    tm = min([ROW_TILE] + [lo.shape[1] for lo in ya_los])
    tiles = rows // tm
    seq_tiles = tuple(2 * lo.shape[1] // tm for lo in ya_los)
    rpt = d // LANES

    def tile_of(q, req, tile):
        return jnp.where(req == q, tile, jnp.where(req < q, 0, tiles - 1))

    args, in_specs = [], []
    for q in range(n_req):
        n_p = seq_tiles[q]
        rows_spec = lambda w, q=q: pl.BlockSpec(
            (tm, w), lambda req, tile: (tile_of(q, req, tile), 0))
        lo_spec = pl.BlockSpec(
            (None, tm, d_a), lambda req, tile, q=q, n_p=n_p: (
                tile_of(q, req, tile) // n_p,
                jnp.minimum(tile_of(q, req, tile) % n_p, n_p // 2 - 1), 0))
        hi_spec = pl.BlockSpec(
            (None, tm, d_a), lambda req, tile, q=q, n_p=n_p: (
                tile_of(q, req, tile) // n_p,
                jnp.maximum(tile_of(q, req, tile) % n_p - n_p // 2, 0), 0))
        args += [xs[q], ya_los[q], ya_his[q], ybs[q]]
        in_specs += [rows_spec(d), lo_spec, hi_spec, rows_spec(ybs[q].shape[1])]
    vec = _whole((1, d))
    col = pl.BlockSpec((TOP_K, tm), lambda req, tile: (0, req * tiles + tile))
    return pl.pallas_call(
        functools.partial(_mix_out_body, alpha=alpha, d_a=d_a, seq_tiles=seq_tiles),
        grid=(n_req, tiles),
        in_specs=in_specs + [vec, vec, _whole(w_o.shape), vec, vec, _whole(rw_hi.shape),
                             _whole(rw_lo.shape), _whole((1, n_exp))],
        out_specs=[pl.BlockSpec((tm * rpt, LANES), lambda req, tile: (req * tiles + tile, 0)),
                   col, col],
        out_shape=[jax.ShapeDtypeStruct((n_req * rows * rpt, LANES), F32),
                   jax.ShapeDtypeStruct((TOP_K, n_req * rows), jnp.int32),
                   jax.ShapeDtypeStruct((TOP_K, n_req * rows), F32)],
        compiler_params=_params(("parallel", "parallel")),
        name="mix_out",
    )(*args, g0, b0, w_o, g1, b1, rw_hi, rw_lo, rb)


def _column_streams(w2d, tr):
    rows, cols = w2d.shape
    n = PREP_STREAMS if cols % (PREP_STREAMS * SPLIT_GROUP) == 0 else 1
    specs = [pl.BlockSpec((tr, cols // n), functools.partial(lambda i, q: (i, q), q=q))
             for q in range(n)]
    return [w2d] * n, specs


def _split_gu_body(*refs):
    *w_refs, perm_ref, g_ref, l_ref = refs
    half = SPLIT_GROUP // 2
    groups = w_refs[0].shape[1] // SPLIT_GROUP
    for q, w_ref in enumerate(w_refs):
        for c in range(groups):
            w = w_ref[:, c * SPLIT_GROUP:(c + 1) * SPLIT_GROUP].astype(BF16)
            r = _dot(w, perm_ref[...])
            o = (q * groups + c) * half
            g_ref[:, o:o + half] = r[:, :half].astype(BF16)
            l_ref[:, o:o + half] = r[:, half:].astype(BF16)


def _split_gu(w2d):
    rows, cols = w2d.shape
    assert cols % SPLIT_GROUP == 0
    tr = min(PREP_ROWS, rows)
    perm = np.zeros((SPLIT_GROUP, SPLIT_GROUP), np.float32)
    j = np.arange(SPLIT_GROUP // 2)
    perm[2 * j, j] = 1.0
    perm[2 * j + 1, SPLIT_GROUP // 2 + j] = 1.0
    streams, specs = _column_streams(w2d, tr)
    out = jax.ShapeDtypeStruct((rows, cols // 2), BF16)
    return pl.pallas_call(
        _split_gu_body,
        grid=(rows // tr,),
        in_specs=specs + [_whole(perm.shape)],
        out_specs=[pl.BlockSpec((tr, cols // 2), lambda i: (i, 0))] * 2,
        out_shape=[out, out],
        compiler_params=_params(("parallel",)),
        name="split_gu",
    )(*streams, jnp.asarray(perm, dtype=BF16))


def _cast_body(*refs):
    *w_refs, o_ref = refs
    width = w_refs[0].shape[1]
    for q, w_ref in enumerate(w_refs):
        o_ref[:, q * width:(q + 1) * width] = w_ref[...].astype(BF16)


def _cast_bf16(w2d):
    rows, cols = w2d.shape
    tr = min(PREP_ROWS, rows)
    streams, specs = _column_streams(w2d, tr)
    return pl.pallas_call(
        _cast_body,
        grid=(rows // tr,),
        in_specs=specs,
        out_specs=pl.BlockSpec((tr, cols), lambda i: (i, 0)),
        out_shape=jax.ShapeDtypeStruct((rows, cols), BF16),
        compiler_params=_params(("parallel",)),
        name="cast_bf16",
    )(*streams)


def _moe_body(be_ref, nv_ref, src_ref, src_next_ref, dst_ref, dst_prev_ref, x_hbm, wg_hbm,
              wl_hbm, bg_ref, bl_ref, wd_hbm, bd_ref, y_hbm, xbuf_ref, xs_ref, act_ref, obuf_ref,
              wg_ref, wl_ref, wd_ref, gsem, ssem, wsem):
    i, n_valid = pl.program_id(0), nv_ref[0]
    e = be_ref[i]
    e_next = be_ref[jnp.minimum(i + 1, pl.num_programs(0) - 1)]
    first_of_expert = jnp.logical_or(i == 0, be_ref[jnp.maximum(i - 1, 0)] != e)
    last_of_expert = jnp.logical_and(i + 1 < n_valid, e_next != e)

    def up_copies(expert):
        return (pltpu.make_async_copy(wg_hbm.at[expert], wg_ref, wsem.at[0]),
                pltpu.make_async_copy(wl_hbm.at[expert], wl_ref, wsem.at[1]))

    def down_copy(expert):
        return pltpu.make_async_copy(wd_hbm.at[expert], wd_ref, wsem.at[2])

    bm, d = xs_ref.shape
    rpt = d // LANES
    pitch = obuf_ref.shape[0] // bm
    slot = i % 2

    def gather_copy(idx_ref, r, to_slot):
        tok = idx_ref[0, 0, r]
        return pltpu.make_async_copy(x_hbm.at[pl.ds(pl.multiple_of(tok * rpt, rpt), rpt), :],
                                     xbuf_ref.at[to_slot, pl.ds(r * pitch, rpt), :], gsem)

    def scatter_copy(idx_ref, r):
        row = idx_ref[0, 0, r]
        return pltpu.make_async_copy(obuf_ref.at[pl.ds(r * pitch, rpt), :],
                                     y_hbm.at[pl.ds(pl.multiple_of(row * rpt, rpt), rpt), :], ssem)

    def wait_gather(of_slot):
        pltpu.make_async_copy(x_hbm.at[pl.ds(0, bm * rpt), :],
                              xbuf_ref.at[of_slot, pl.ds(0, bm * rpt), :], gsem).wait()

    def wait_scatter():
        pltpu.make_async_copy(obuf_ref.at[pl.ds(0, bm * rpt), :],
                              y_hbm.at[pl.ds(0, bm * rpt), :], ssem).wait()

    @pl.when(i < n_valid)
    def _():
        @pl.when(i == 0)
        def _():
            obuf_ref[...] = jnp.zeros_like(obuf_ref)
            for r in range(bm):
                gather_copy(src_ref, r, 0).start(priority=r % 2)
            for copy in up_copies(e):
                copy.start()

        @pl.when(first_of_expert)
        def _():
            down_copy(e).start()
            for copy in up_copies(e):
                copy.wait()

        wait_gather(slot)
        for c, part in enumerate(_load_token_rows(xbuf_ref.at[slot], bm, rpt, pitch)):
            xs_ref[:, c * LANES:(c + 1) * LANES] = part.astype(BF16)
        x = xs_ref[...]
        d_ff = wg_ref.shape[1]
        tf = min(MOE_FF_TILE, d_ff)
        n_ff = d_ff // tf
        for f in range(n_ff):
            fs = slice(f * tf, (f + 1) * tf)
            glu = jnp.minimum(_dot(x, wg_ref[:, fs]) + bg_ref[0, :, fs], SWIGLU_LIMIT)
            lin = jnp.clip(_dot(x, wl_ref[:, fs]) + bl_ref[0, :, fs],
                           -SWIGLU_LIMIT, SWIGLU_LIMIT)
            for r in range(f * bm // n_ff, (f + 1) * bm // n_ff):
                gather_copy(src_next_ref, r, 1 - slot).start(priority=r % 2)
                scatter_copy(dst_prev_ref, r).start(priority=r % 2)
            act = glu * (1.0 / (1.0 + jnp.exp(-SWIGLU_ALPHA * glu))) * (lin + 1.0)
            act_ref[:, fs] = act.astype(BF16)

        wait_scatter()

        @pl.when(last_of_expert)
        def _():
            for copy in up_copies(e_next):
                copy.start()

        @pl.when(first_of_expert)
        def _():
            down_copy(e).wait()

        tn = min(MOE_OUT_TILE, d)
        for n in range(d // tn):
            ns = slice(n * tn, (n + 1) * tn)
            done = _dot(act_ref[...], wd_ref[:, ns]) + bd_ref[0, :, ns]
            for c in range(tn // LANES):
                obuf_ref[pl.ds(n * (tn // LANES) + c, bm, stride=pitch), :] = (
                    done[:, c * LANES:(c + 1) * LANES])

        @pl.when(i == n_valid - 1)
        def _():
            for r in range(bm):
                scatter_copy(dst_ref, r).start(priority=r % 2)
            wait_gather(1 - slot)
            wait_scatter()


def _moe(blk_e, n_valid, src, dst, x1r, w_g, w_l, b_g, b_l, w_d, b_d, n_out_rows):
    n_blocks, _, bm = src.shape
    n_exp, d, d_ff = w_g.shape
    rpt = d // LANES
    pitch = rpt + MOE_ROW_PAD if rpt % 16 == 0 else rpt
    expert = lambda i, be, nv: (be[i], 0, 0)
    idx_block = lambda index_map: pl.BlockSpec((1, 1, bm), index_map, memory_space=pltpu.SMEM)
    grid_spec = pltpu.PrefetchScalarGridSpec(
        num_scalar_prefetch=2,
        grid=(n_blocks,),
        in_specs=[
            idx_block(lambda i, be, nv: (i, 0, 0)),
            idx_block(lambda i, be, nv: (jnp.minimum(i + 1, n_blocks - 1), 0, 0)),
            idx_block(lambda i, be, nv: (i, 0, 0)),
            idx_block(lambda i, be, nv: (jnp.maximum(i - 1, 0), 0, 0)),
            pl.BlockSpec(memory_space=pl.ANY),
            pl.BlockSpec(memory_space=pl.ANY),
            pl.BlockSpec(memory_space=pl.ANY),
            pl.BlockSpec((1, 1, d_ff), expert),
            pl.BlockSpec((1, 1, d_ff), expert),
            pl.BlockSpec(memory_space=pl.ANY),
            pl.BlockSpec((1, 1, d), expert),
        ],
        out_specs=pl.BlockSpec(memory_space=pl.ANY),
        scratch_shapes=[pltpu.VMEM((2, bm * pitch, LANES), F32), pltpu.VMEM((bm, d), BF16),
                        pltpu.VMEM((bm, d_ff), BF16), pltpu.VMEM((bm * pitch, LANES), F32),
                        pltpu.VMEM((d, d_ff), BF16), pltpu.VMEM((d, d_ff), BF16),
                        pltpu.VMEM((d_ff, d), BF16),
                        pltpu.SemaphoreType.DMA(()), pltpu.SemaphoreType.DMA(()),
                        pltpu.SemaphoreType.DMA((3,))],
    )
    return pl.pallas_call(
        _moe_body,
        grid_spec=grid_spec,
        out_shape=jax.ShapeDtypeStruct((n_out_rows * rpt, LANES), F32),
        compiler_params=_params(("arbitrary",)),
        name="moe",
    )(blk_e, n_valid, src, src, dst, dst, x1r, w_g, w_l, b_g, b_l, w_d, b_d)


def _ln_out_body(x1r_ref, gate_ref, *refs, alpha):
    *y_refs, g_ref, b_ref, o_ref = refs
    tm, d = o_ref.shape
    rpt = d // LANES
    gates = gate_ref[...]
    parts = [alpha * p for p in _load_token_rows(x1r_ref, tm, rpt)]
    for c, y_ref in enumerate(y_refs):
        gate = jnp.broadcast_to(gates[:, c:c + 1], (tm, LANES))
        parts = [p + gate * y for p, y in zip(parts, _load_token_rows(y_ref, tm, rpt))]
    o_ref[...] = _layer_norm(jnp.concatenate(parts, axis=1), g_ref[...], b_ref[...])


def _ln_out(x1r, y_rows, gates, g, b, alpha, rows, row0, total_rows):
    d = g.shape[1]
    rpt = d // LANES
    tm = min(ROW_TILE, rows)
    assert row0 % tm == 0 and total_rows % tm == 0
    tile0 = row0 // tm
    tok = lambda off: pl.BlockSpec((tm * rpt, LANES), lambda i: (off + tile0 + i, 0))
    return pl.pallas_call(
        functools.partial(_ln_out_body, alpha=alpha),
        grid=(rows // tm,),
        in_specs=[tok(0), pl.BlockSpec((tm, TOP_K), lambda i: (tile0 + i, 0))]
                 + [tok(c * (total_rows // tm)) for c in range(TOP_K)]
                 + [_whole((1, d)), _whole((1, d))],
        out_specs=pl.BlockSpec((tm, d), lambda i: (i, 0)),
        out_shape=jax.ShapeDtypeStruct((rows, d), F32),
        compiler_params=_params(("parallel",)),
        name="ln_out",
    )(x1r, gates, *([y_rows] * TOP_K), g, b)


def _route(top_idx, n_exp, bm):
    t = top_idx.shape[0]
    tk = t * TOP_K
    sorted_e, order = lax.sort_key_val(top_idx.reshape(-1), jnp.arange(tk, dtype=jnp.int32))
    experts = jnp.arange(n_exp, dtype=jnp.int32)
    starts = jnp.sum(sorted_e[None, :] < experts[:, None], axis=1, dtype=jnp.int32)
    counts = jnp.sum(sorted_e[None, :] == experts[:, None], axis=1, dtype=jnp.int32)
    padded = ((counts + bm - 1) // bm) * bm
    pends = jnp.cumsum(padded)
    pstarts = pends - padded
    n_blocks = (tk + n_exp * bm) // bm
    blk_start = jnp.arange(n_blocks, dtype=jnp.int32) * bm
    blk_e = jnp.minimum(jnp.sum(pends[None, :] <= blk_start[:, None], axis=1, dtype=jnp.int32),
                        n_exp - 1)
    n_valid = (pends[-1:] // bm).astype(jnp.int32)
    lane = jnp.arange(bm, dtype=jnp.int32)[None, :]
    row_in_expert = (blk_start - pstarts[blk_e])[:, None] + lane
    valid = row_in_expert < counts[blk_e][:, None]
    pick = order[jnp.clip(starts[blk_e][:, None] + row_in_expert, 0, tk - 1)]
    tok, choice = pick // TOP_K, pick % TOP_K
    src = jnp.where(valid, tok, 0)
    spare = tk + blk_start[:, None] + lane - (starts + counts)[blk_e][:, None]
    dst = jnp.where(valid, choice * t + tok, spare)
    return src[:, None, :], dst[:, None, :], blk_e, n_valid


def _channel_dft_table():
    n = np.arange(HEAD)
    ang = 2.0 * np.pi * ((n[:, None] * n[None, :]) % HEAD) / HEAD
    return jnp.asarray(np.concatenate([np.cos(ang), np.sin(ang)], axis=1), dtype=BF16)


def kernel(x_prompt, x_sample, ln_in_g, ln_in_b, w_in, sgu_norm_g, sgu_norm_b, sgu_w, sgu_b,
           w_o, ln1_g, ln1_b, router_w, router_b, w_gu, b_gu, w_down, b_down, ln2_g, ln2_b):
    depth, d, d_in = w_in.shape
    assert depth == 1, "single-layer encoder only"
    n_heads_b, chunk, _ = sgu_w.shape[1:]
    assert chunk == HEAD
    d_b = n_heads_b * HEAD
    d_a = d_in - 2 * d_b
    n_exp = router_w.shape[2]
    d_ff = w_down.shape[2]
    alpha = (2.0 * depth) ** 0.25
    vec = lambda p: p.reshape(1, -1)

    w_in_b = w_in[0].astype(BF16)
    w_o_b = w_o[0].astype(BF16)
    sgu_w_b = sgu_w[0].astype(BF16)
    sgu_bias = jnp.broadcast_to(sgu_b[0][:, :, None], (n_heads_b, HEAD, HEAD))
    cs = _channel_dft_table()
    rw_hi = router_w[0].astype(BF16)
    rw_lo = (router_w[0] - rw_hi.astype(F32)).astype(BF16)
    w_g, w_l = _split_gu(w_gu[0].reshape(n_exp * d, 2 * d_ff))
    w_g = w_g.reshape(n_exp, d, d_ff)
    w_l = w_l.reshape(n_exp, d, d_ff)
    b_g = b_gu[0][:, None, 0::2]
    b_l = b_gu[0][:, None, 1::2]
    w_d = _cast_bf16(w_down[0].reshape(n_exp * d_ff, d)).reshape(n_exp, d_ff, d)
    b_d = b_down[0][:, None, :]

    requests = (x_prompt, x_sample)
    total_rows = sum(x.shape[0] * x.shape[1] for x in requests)
    xs, ya_los, ya_his, ybs = [], [], [], []
    for x in requests:
        n_batch, seq, _ = x.shape
        x2d = x.reshape(n_batch * seq, d)
        a, q, yb = _mix_in(x2d, vec(ln_in_g), vec(ln_in_b), w_in_b, cs, vec(sgu_norm_g[0]),
                           vec(sgu_norm_b[0]), sgu_w_b, sgu_bias, d_a, d_b)
        ya_lo, ya_hi = _seq_dft(a.reshape(n_batch, seq, d_a), q.reshape(n_batch, seq, d_a))
        xs.append(x2d)
        ya_los.append(ya_lo)
        ya_his.append(ya_hi)
        ybs.append(yb)
    x1r, top_idx, top_gates = _mix_out(xs, ya_los, ya_his, ybs, vec(ln_in_g), vec(ln_in_b),
                                       w_o_b, vec(ln1_g[0]), vec(ln1_b[0]), rw_hi, rw_lo,
                                       vec(router_b[0]), alpha)

    bm = min(MOE_ROWS, total_rows * TOP_K)
    gates = top_gates.T
    src, dst, blk_e, n_valid = _route(top_idx.T, n_exp, bm)
    y_rows = _moe(blk_e, n_valid, src, dst, x1r, w_g, w_l, b_g, b_l, w_d, b_d,
                  total_rows * TOP_K + n_exp * bm)

    outs, row0 = [], 0
    for x in requests:
        rows = x.shape[0] * x.shape[1]
        out = _ln_out(x1r, y_rows, gates, vec(ln2_g[0]), vec(ln2_b[0]), alpha,
                      rows, row0, total_rows)
        outs.append(out.reshape(x.shape))
        row0 += rows
    return tuple(outs)
```

```python
import functools
import math

import jax
import jax.numpy as jnp
import numpy as np
from jax import lax
from jax.experimental import pallas as pl
from jax.experimental.pallas import tpu as pltpu

F32 = jnp.float32
BF16 = jnp.bfloat16

LN_EPS = 1e-5
TOP_K = 4
HEAD = 128
LANES = 128
SWIGLU_LIMIT = 7.0
SWIGLU_ALPHA = 1.702
VMEM_LIMIT_BYTES = 56 * 1024 * 1024

ROW_TILE = 256
MIX_IN_ROWS = 512
DFT_K_TILE = 512
DFT_S_TILE = 1024
DFT_ACC_BYTES = 9 * 1024 * 1024
MOE_ROWS = 512
MOE_FF_TILE = 512
MOE_OUT_TILE = 512
MOE_ROW_PAD = 8
PREP_ROWS = 512
SPLIT_GROUP = 512
PREP_STREAMS = 4


def _layer_norm(x, g, b):
    mu = jnp.mean(x, axis=-1, keepdims=True)
    xc = x - mu
    var = jnp.mean(xc * xc, axis=-1, keepdims=True)
    return xc * lax.rsqrt(var + LN_EPS) * g + b


def _gelu(z):
    return 0.5 * z * (1.0 + lax.erf(z * (1.0 / math.sqrt(2.0))))


def _dot(a, b):
    return jnp.dot(a, b, preferred_element_type=F32)


def _params(semantics):
    return pltpu.CompilerParams(dimension_semantics=semantics,
                                vmem_limit_bytes=VMEM_LIMIT_BYTES)


def _whole(shape):
    return pl.BlockSpec(shape, lambda *_: (0,) * len(shape),
                        pipeline_mode=pl.Buffered(1))


def _store_token_rows(ref, val):
    rpt = val.shape[1] // LANES
    for c in range(rpt):
        ref[pl.ds(c, val.shape[0], stride=rpt), :] = val[:, c * LANES:(c + 1) * LANES]


def _load_token_rows(ref, n_tokens, rpt, pitch=None):
    return [ref[pl.ds(c, n_tokens, stride=pitch or rpt), :] for c in range(rpt)]


def _mix_in_body(x_ref, g_ref, b_ref, win_ref, cs_ref, sg_ref, sb_ref, sw_ref, sbias_ref,
                 a_ref, q_ref, yb_ref, *, d_a, d_b):
    xb = _layer_norm(x_ref[...], g_ref[...], b_ref[...]).astype(BF16)
    rows = xb.shape[0]

    za = _dot(xb, win_ref[:, :d_a]).astype(BF16)
    for h in range(d_a // HEAD):
        hs = slice(h * HEAD, (h + 1) * HEAD)
        ab = _dot(za[:, hs], cs_ref[...])
        a_ref[:, hs] = ab[:, :HEAD].astype(BF16)
        q_ref[:, hs] = ab[:, HEAD:].astype(BF16)

    u = _gelu(_dot(xb, win_ref[:, d_a:d_a + d_b]))
    v = _gelu(_dot(xb, win_ref[:, d_a + d_b:]))
    for h in range(d_b // HEAD):
        hs = slice(h * HEAD, (h + 1) * HEAD)
        vn = _layer_norm(v[:, hs], sg_ref[:, hs], sb_ref[:, hs]).astype(BF16)
        for c in range(rows // HEAD):
            rs = slice(c * HEAD, (c + 1) * HEAD)
            gate = _dot(sw_ref[h], vn[rs, :]) + sbias_ref[h]
            yb_ref[rs, hs] = (u[rs, hs] * gate).astype(BF16)


def _mix_in(x2d, ln_g, ln_b, w_in, cs, sgu_g, sgu_b, sgu_w, sgu_bias, d_a, d_b):
    rows, d = x2d.shape
    tm = min(MIX_IN_ROWS, rows)
    n_heads_b = d_b // HEAD
    row = lambda w: pl.BlockSpec((tm, w), lambda i: (i, 0))
    return pl.pallas_call(
        functools.partial(_mix_in_body, d_a=d_a, d_b=d_b),
        grid=(rows // tm,),
        in_specs=[row(d), _whole((1, d)), _whole((1, d)), _whole(w_in.shape),
                  _whole(cs.shape), _whole((1, d_b)), _whole((1, d_b)),
                  _whole((n_heads_b, HEAD, HEAD)), _whole((n_heads_b, HEAD, HEAD))],
        out_specs=[row(d_a), row(d_a), row(d_b)],
        out_shape=[jax.ShapeDtypeStruct((rows, d_a), BF16),
                   jax.ShapeDtypeStruct((rows, d_a), BF16),
                   jax.ShapeDtypeStruct((rows, d_b), BF16)],
        compiler_params=_params(("parallel",)),
        name="mix_in",
    )(x2d, ln_g, ln_b, w_in, cs, sgu_g, sgu_b, sgu_w, sgu_bias)


def _seq_dft_body(a_ref, q_ref, c0_ref, s0_ref, perm_ref, lo_ref, hi_ref,
                  accp_ref, accz_ref, cb_ref, sb_ref, ca_ref, sa_ref, cd_ref, sd_ref,
                  *, seq, scale):
    i, j = pl.program_id(1), pl.program_id(2)
    n_batch, tk, _ = lo_ref.shape
    tkx, ts = cb_ref.shape
    w = 2.0 * math.pi / seq

    @pl.when(j == 0)
    def _():
        accp_ref[...] = jnp.zeros_like(accp_ref)
        accz_ref[...] = jnp.zeros_like(accz_ref)
        ds = lax.broadcasted_iota(jnp.int32, (1, ts), 1)
        ph = (((i * tk) * ds) & (seq - 1)).astype(F32) * w
        ci, si = jnp.cos(ph), jnp.sin(ph)
        cb_ref[...] = ci * c0_ref[...] - si * s0_ref[...]
        sb_ref[...] = si * c0_ref[...] + ci * s0_ref[...]
        k = i * tk + lax.broadcasted_iota(jnp.int32, (tkx, HEAD), 0)
        pd = ((k * ts) & (seq - 1)).astype(F32) * w
        cd_ref[...] = jnp.cos(pd)
        sd_ref[...] = jnp.sin(pd)
        ca_ref[...] = jnp.ones_like(ca_ref)
        sa_ref[...] = jnp.zeros_like(sa_ref)

    ca, sa = ca_ref[...], sa_ref[...]
    cos_parts, sin_parts = [], []
    for g in range(ts // HEAD):
        gs = slice(g * HEAD, (g + 1) * HEAD)
        cb, sb = cb_ref[:, gs], sb_ref[:, gs]
        cos_parts.append((ca * cb - sa * sb).astype(BF16))
        sin_parts.append((sa * cb + ca * sb).astype(BF16))
    cos_t = jnp.concatenate(cos_parts, axis=1)
    sin_t = jnp.concatenate(sin_parts, axis=1)
    for b in range(n_batch):
        accp_ref[b] += _dot(cos_t, a_ref[b])
        accz_ref[b] += _dot(sin_t, q_ref[b])
    cd, sd = cd_ref[...], sd_ref[...]
    ca_ref[...] = ca * cd - sa * sd
    sa_ref[...] = sa * cd + ca * sd

    @pl.when(j == pl.num_programs(2) - 1)
    def _():
        first_row = lax.broadcasted_iota(jnp.int32, (tk, 1), 0) == 0
        for b in range(n_batch):
            p, z = accp_ref[b], accz_ref[b]
            lo_ref[b] = ((p[:tk] - z[:tk]) * scale).astype(BF16)
            mirror = ((p + z) * scale).astype(BF16)
            rev = _dot(perm_ref[...], mirror[:tk]).astype(BF16)
            hi_ref[b] = jnp.where(first_row, mirror[tk:tk + 1], rev)


def _seq_dft(a3d, q3d):
    n_batch, seq, d_a = a3d.shape
    assert seq & (seq - 1) == 0, "sequence length must be a power of two"
    half = seq // 2
    tk, ts = min(DFT_K_TILE, half), min(DFT_S_TILE, seq)
    tkx = tk + 8
    tc = d_a
    while tc > LANES and 2 * n_batch * tkx * tc * 4 > DFT_ACC_BYTES:
        tc //= 2
    n_i = half // tk
    scale = 1.0 / math.sqrt(seq * HEAD)
    ang = 2.0 * np.pi * np.outer(np.arange(tkx), np.arange(ts)) / seq
    c0 = jnp.asarray(np.cos(ang), dtype=F32)
    s0 = jnp.asarray(np.sin(ang), dtype=F32)
    perm = np.zeros((tk, tk), np.float32)
    r = np.arange(1, tk)
    perm[r, tk - r] = 1.0
    in_spec = pl.BlockSpec((n_batch, ts, tc), lambda c, i, j: (0, j, c))
    out = jax.ShapeDtypeStruct((n_batch, half, d_a), BF16)
    return pl.pallas_call(
        functools.partial(_seq_dft_body, seq=seq, scale=scale),
        grid=(d_a // tc, n_i, seq // ts),
        in_specs=[in_spec, in_spec, _whole((tkx, ts)), _whole((tkx, ts)), _whole((tk, tk))],
        out_specs=[pl.BlockSpec((n_batch, tk, tc), lambda c, i, j: (0, i, c)),
                   pl.BlockSpec((n_batch, tk, tc), lambda c, i, j: (0, n_i - 1 - i, c))],
        out_shape=[out, out],
        scratch_shapes=[pltpu.VMEM((n_batch, tkx, tc), F32), pltpu.VMEM((n_batch, tkx, tc), F32),
                        pltpu.VMEM((tkx, ts), F32), pltpu.VMEM((tkx, ts), F32)]
                       + [pltpu.VMEM((tkx, HEAD), F32)] * 4,
        compiler_params=_params(("parallel", "parallel", "arbitrary")),
        name="seq_dft",
    )(a3d, q3d, c0, s0, jnp.asarray(perm, dtype=BF16))


def _top_k_gates(logits_t):
    n_exp, n_tok = logits_t.shape
    expert = lax.broadcasted_iota(jnp.int32, (n_exp, n_tok), 0)
    slot = lax.broadcasted_iota(jnp.int32, (TOP_K, n_tok), 0)
    work = logits_t
    top = jnp.zeros((TOP_K, n_tok), F32)
    idx = jnp.zeros((TOP_K, n_tok), jnp.int32)
    for c in range(TOP_K):
        best = jnp.max(work, axis=0, keepdims=True)
        where = jnp.min(jnp.where(work == best, expert, n_exp), axis=0, keepdims=True)
        top = jnp.where(slot == c, best, top)
        idx = jnp.where(slot == c, where, idx)
        work = jnp.where(expert == where, -jnp.inf, work)
    e = jnp.exp(top - jnp.max(top, axis=0, keepdims=True))
    return idx, e / jnp.sum(e, axis=0, keepdims=True)


def _mix_out_body(*refs, alpha, d_a, seq_tiles):
    n_req = len(seq_tiles)
    per_req = [refs[4 * q:4 * q + 4] for q in range(n_req)]
    (g0_ref, b0_ref, wo_ref, g1_ref, b1_ref, rwh_ref, rwl_ref, rb_ref,
     x1r_ref, idx_ref, gate_ref) = refs[4 * n_req:]
    req, tile = pl.program_id(0), pl.program_id(1)

    def active(k):
        val = per_req[0][k][...]
        for q in range(1, n_req):
            val = jnp.where(req == q, per_req[q][k][...], val)
        return val

    n_p = seq_tiles[0]
    for q in range(1, n_req):
        n_p = jnp.where(req == q, seq_tiles[q], n_p)
    in_low_half = tile % n_p < n_p // 2
    xln = _layer_norm(active(0), g0_ref[...], b0_ref[...])
    ya = jnp.where(in_low_half, active(1), active(2))
    mix = _dot(ya, wo_ref[:d_a, :]) + _dot(active(3), wo_ref[d_a:, :])
    x1 = _layer_norm(alpha * xln + mix, g1_ref[...], b1_ref[...])
    _store_token_rows(x1r_ref, x1)
    hi = x1.astype(BF16)
    lo = (x1 - hi.astype(F32)).astype(BF16)
    logits = (_dot(hi, rwh_ref[...]) + _dot(hi, rwl_ref[...])
              + _dot(lo, rwh_ref[...]) + rb_ref[...])
    idx_ref[...], gate_ref[...] = _top_k_gates(logits.T)


def _mix_out(xs, ya_los, ya_his, ybs, g0, b0, w_o, g1, b1, rw_hi, rw_lo, rb, alpha):
    n_req = len(xs)
    rows, d = xs[0].shape
    assert all(x.shape == (rows, d) for x in xs)
    d_a = ya_los[0].shape[2]
    n_exp = rw_hi.shape[1]
    tm = min([ROW_TILE] + [lo.shape[1] for lo in ya_los])
    tiles = rows // tm
    seq_tiles = tuple(2 * lo.shape[1] // tm for lo in ya_los)
    rpt = d // LANES

    def tile_of(q, req, tile):
        return jnp.where(req == q, tile, jnp.where(req < q, 0, tiles - 1))

    args, in_specs = [], []
    for q in range(n_req):
        n_p = seq_tiles[q]
        rows_spec = lambda w, q=q: pl.BlockSpec(
            (tm, w), lambda req, tile: (tile_of(q, req, tile), 0))
        lo_spec = pl.BlockSpec(
            (None, tm, d_a), lambda req, tile, q=q, n_p=n_p: (
                tile_of(q, req, tile) // n_p,
                jnp.minimum(tile_of(q, req, tile) % n_p, n_p // 2 - 1), 0))
        hi_spec = pl.BlockSpec(
            (None, tm, d_a), lambda req, tile, q=q, n_p=n_p: (
                tile_of(q, req, tile) // n_p,
                jnp.maximum(tile_of(q, req, tile) % n_p - n_p // 2, 0), 0))
        args += [xs[q], ya_los[q], ya_his[q], ybs[q]]
        in_specs += [rows_spec(d), lo_spec, hi_spec, rows_spec(ybs[q].shape[1])]
    vec = _whole((1, d))
    col = pl.BlockSpec((TOP_K, tm), lambda req, tile: (0, req * tiles + tile))
    return pl.pallas_call(
        functools.partial(_mix_out_body, alpha=alpha, d_a=d_a, seq_tiles=seq_tiles),
        grid=(n_req, tiles),
        in_specs=in_specs + [vec, vec, _whole(w_o.shape), vec, vec, _whole(rw_hi.shape),
                             _whole(rw_lo.shape), _whole((1, n_exp))],
        out_specs=[pl.BlockSpec((tm * rpt, LANES), lambda req, tile: (req * tiles + tile, 0)),
                   col, col],
        out_shape=[jax.ShapeDtypeStruct((n_req * rows * rpt, LANES), F32),
                   jax.ShapeDtypeStruct((TOP_K, n_req * rows), jnp.int32),
                   jax.ShapeDtypeStruct((TOP_K, n_req * rows), F32)],
        compiler_params=_params(("parallel", "parallel")),
        name="mix_out",
    )(*args, g0, b0, w_o, g1, b1, rw_hi, rw_lo, rb)


def _column_streams(w2d, tr):
    rows, cols = w2d.shape
    n = PREP_STREAMS if cols % (PREP_STREAMS * SPLIT_GROUP) == 0 else 1
    specs = [pl.BlockSpec((tr, cols // n), functools.partial(lambda i, q: (i, q), q=q))
             for q in range(n)]
    return [w2d] * n, specs


def _split_gu_body(*refs):
    *w_refs, perm_ref, g_ref, l_ref = refs
    half = SPLIT_GROUP // 2
    groups = w_refs[0].shape[1] // SPLIT_GROUP
    for q, w_ref in enumerate(w_refs):
        for c in range(groups):
            w = w_ref[:, c * SPLIT_GROUP:(c + 1) * SPLIT_GROUP].astype(BF16)
            r = _dot(w, perm_ref[...])
            o = (q * groups + c) * half
            g_ref[:, o:o + half] = r[:, :half].astype(BF16)
            l_ref[:, o:o + half] = r[:, half:].astype(BF16)


def _split_gu(w2d):
    rows, cols = w2d.shape
    assert cols % SPLIT_GROUP == 0
    tr = min(PREP_ROWS, rows)
    perm = np.zeros((SPLIT_GROUP, SPLIT_GROUP), np.float32)
    j = np.arange(SPLIT_GROUP // 2)
    perm[2 * j, j] = 1.0
    perm[2 * j + 1, SPLIT_GROUP // 2 + j] = 1.0
    streams, specs = _column_streams(w2d, tr)
    out = jax.ShapeDtypeStruct((rows, cols // 2), BF16)
    return pl.pallas_call(
        _split_gu_body,
        grid=(rows // tr,),
        in_specs=specs + [_whole(perm.shape)],
        out_specs=[pl.BlockSpec((tr, cols // 2), lambda i: (i, 0))] * 2,
        out_shape=[out, out],
        compiler_params=_params(("parallel",)),
        name="split_gu",
    )(*streams, jnp.asarray(perm, dtype=BF16))


def _cast_body(*refs):
    *w_refs, o_ref = refs
    width = w_refs[0].shape[1]
    for q, w_ref in enumerate(w_refs):
        o_ref[:, q * width:(q + 1) * width] = w_ref[...].astype(BF16)


def _cast_bf16(w2d):
    rows, cols = w2d.shape
    tr = min(PREP_ROWS, rows)
    streams, specs = _column_streams(w2d, tr)
    return pl.pallas_call(
        _cast_body,
        grid=(rows // tr,),
        in_specs=specs,
        out_specs=pl.BlockSpec((tr, cols), lambda i: (i, 0)),
        out_shape=jax.ShapeDtypeStruct((rows, cols), BF16),
        compiler_params=_params(("parallel",)),
        name="cast_bf16",
    )(*streams)


def _moe_body(be_ref, nv_ref, src_ref, src_next_ref, dst_ref, dst_prev_ref, x_hbm, wg_hbm,
              wl_hbm, bg_ref, bl_ref, wd_hbm, bd_ref, y_hbm, xbuf_ref, xs_ref, act_ref, obuf_ref,
              wg_ref, wl_ref, wd_ref, gsem, ssem, wsem):
    i, n_valid = pl.program_id(0), nv_ref[0]
    e = be_ref[i]
    e_next = be_ref[jnp.minimum(i + 1, pl.num_programs(0) - 1)]
    first_of_expert = jnp.logical_or(i == 0, be_ref[jnp.maximum(i - 1, 0)] != e)
    last_of_expert = jnp.logical_and(i + 1 < n_valid, e_next != e)

    def up_copies(expert):
        return (pltpu.make_async_copy(wg_hbm.at[expert], wg_ref, wsem.at[0]),
                pltpu.make_async_copy(wl_hbm.at[expert], wl_ref, wsem.at[1]))

    def down_copy(expert):
        return pltpu.make_async_copy(wd_hbm.at[expert], wd_ref, wsem.at[2])

    bm, d = xs_ref.shape
    rpt = d // LANES
    pitch = obuf_ref.shape[0] // bm
    slot = i % 2

    def gather_copy(idx_ref, r, to_slot):
        tok = idx_ref[0, 0, r]
        return pltpu.make_async_copy(x_hbm.at[pl.ds(pl.multiple_of(tok * rpt, rpt), rpt), :],
                                     xbuf_ref.at[to_slot, pl.ds(r * pitch, rpt), :], gsem)

    def scatter_copy(idx_ref, r):
        row = idx_ref[0, 0, r]
        return pltpu.make_async_copy(obuf_ref.at[pl.ds(r * pitch, rpt), :],
                                     y_hbm.at[pl.ds(pl.multiple_of(row * rpt, rpt), rpt), :], ssem)

    def wait_gather(of_slot):
        pltpu.make_async_copy(x_hbm.at[pl.ds(0, bm * rpt), :],
                              xbuf_ref.at[of_slot, pl.ds(0, bm * rpt), :], gsem).wait()

    def wait_scatter():
        pltpu.make_async_copy(obuf_ref.at[pl.ds(0, bm * rpt), :],
                              y_hbm.at[pl.ds(0, bm * rpt), :], ssem).wait()

    @pl.when(i < n_valid)
    def _():
        @pl.when(i == 0)
        def _():
            obuf_ref[...] = jnp.zeros_like(obuf_ref)
            for r in range(bm):
                gather_copy(src_ref, r, 0).start(priority=r % 2)
            for copy in up_copies(e):
                copy.start()

        @pl.when(first_of_expert)
        def _():
            down_copy(e).start()
            for copy in up_copies(e):
                copy.wait()

        wait_gather(slot)
        for c, part in enumerate(_load_token_rows(xbuf_ref.at[slot], bm, rpt, pitch)):
            xs_ref[:, c * LANES:(c + 1) * LANES] = part.astype(BF16)
        x = xs_ref[...]
        d_ff = wg_ref.shape[1]
        tf = min(MOE_FF_TILE, d_ff)
        n_ff = d_ff // tf
        for f in range(n_ff):
            fs = slice(f * tf, (f + 1) * tf)
            glu = jnp.minimum(_dot(x, wg_ref[:, fs]) + bg_ref[0, :, fs], SWIGLU_LIMIT)
            lin = jnp.clip(_dot(x, wl_ref[:, fs]) + bl_ref[0, :, fs],
                           -SWIGLU_LIMIT, SWIGLU_LIMIT)
            for r in range(f * bm // n_ff, (f + 1) * bm // n_ff):
                gather_copy(src_next_ref, r, 1 - slot).start(priority=r % 2)
                scatter_copy(dst_prev_ref, r).start(priority=r % 2)
            act = glu * (1.0 / (1.0 + jnp.exp(-SWIGLU_ALPHA * glu))) * (lin + 1.0)
            act_ref[:, fs] = act.astype(BF16)

        wait_scatter()

        @pl.when(last_of_expert)
        def _():
            for copy in up_copies(e_next):
                copy.start()

        @pl.when(first_of_expert)
        def _():
            down_copy(e).wait()

        tn = min(MOE_OUT_TILE, d)
        for n in range(d // tn):
            ns = slice(n * tn, (n + 1) * tn)
            done = _dot(act_ref[...], wd_ref[:, ns]) + bd_ref[0, :, ns]
            for c in range(tn // LANES):
                obuf_ref[pl.ds(n * (tn // LANES) + c, bm, stride=pitch), :] = (
                    done[:, c * LANES:(c + 1) * LANES])

        @pl.when(i == n_valid - 1)
        def _():
            for r in range(bm):
                scatter_copy(dst_ref, r).start(priority=r % 2)
            wait_gather(1 - slot)
            wait_scatter()


def _moe(blk_e, n_valid, src, dst, x1r, w_g, w_l, b_g, b_l, w_d, b_d, n_out_rows):
    n_blocks, _, bm = src.shape
    n_exp, d, d_ff = w_g.shape
    rpt = d // LANES
    pitch = rpt + MOE_ROW_PAD if rpt % 16 == 0 else rpt
    expert = lambda i, be, nv: (be[i], 0, 0)
    idx_block = lambda index_map: pl.BlockSpec((1, 1, bm), index_map, memory_space=pltpu.SMEM)
    grid_spec = pltpu.PrefetchScalarGridSpec(
        num_scalar_prefetch=2,
        grid=(n_blocks,),
        in_specs=[
            idx_block(lambda i, be, nv: (i, 0, 0)),
            idx_block(lambda i, be, nv: (jnp.minimum(i + 1, n_blocks - 1), 0, 0)),
            idx_block(lambda i, be, nv: (i, 0, 0)),
            idx_block(lambda i, be, nv: (jnp.maximum(i - 1, 0), 0, 0)),
            pl.BlockSpec(memory_space=pl.ANY),
            pl.BlockSpec(memory_space=pl.ANY),
            pl.BlockSpec(memory_space=pl.ANY),
            pl.BlockSpec((1, 1, d_ff), expert),
            pl.BlockSpec((1, 1, d_ff), expert),
            pl.BlockSpec(memory_space=pl.ANY),
            pl.BlockSpec((1, 1, d), expert),
        ],
        out_specs=pl.BlockSpec(memory_space=pl.ANY),
        scratch_shapes=[pltpu.VMEM((2, bm * pitch, LANES), F32), pltpu.VMEM((bm, d), BF16),
                        pltpu.VMEM((bm, d_ff), BF16), pltpu.VMEM((bm * pitch, LANES), F32),
                        pltpu.VMEM((d, d_ff), BF16), pltpu.VMEM((d, d_ff), BF16),
                        pltpu.VMEM((d_ff, d), BF16),
                        pltpu.SemaphoreType.DMA(()), pltpu.SemaphoreType.DMA(()),
                        pltpu.SemaphoreType.DMA((3,))],
    )
    return pl.pallas_call(
        _moe_body,
        grid_spec=grid_spec,
        out_shape=jax.ShapeDtypeStruct((n_out_rows * rpt, LANES), F32),
        compiler_params=_params(("arbitrary",)),
        name="moe",
    )(blk_e, n_valid, src, src, dst, dst, x1r, w_g, w_l, b_g, b_l, w_d, b_d)


def _ln_out_body(x1r_ref, gate_ref, *refs, alpha):
    *y_refs, g_ref, b_ref, o_ref = refs
    tm, d = o_ref.shape
    rpt = d // LANES
    gates = gate_ref[...]
    parts = [alpha * p for p in _load_token_rows(x1r_ref, tm, rpt)]
    for c, y_ref in enumerate(y_refs):
        gate = jnp.broadcast_to(gates[:, c:c + 1], (tm, LANES))
        parts = [p + gate * y for p, y in zip(parts, _load_token_rows(y_ref, tm, rpt))]
    o_ref[...] = _layer_norm(jnp.concatenate(parts, axis=1), g_ref[...], b_ref[...])


def _ln_out(x1r, y_rows, gates, g, b, alpha, rows, row0, total_rows):
    d = g.shape[1]
    rpt = d // LANES
    tm = min(ROW_TILE, rows)
    assert row0 % tm == 0 and total_rows % tm == 0
    tile0 = row0 // tm
    tok = lambda off: pl.BlockSpec((tm * rpt, LANES), lambda i: (off + tile0 + i, 0))
    return pl.pallas_call(
        functools.partial(_ln_out_body, alpha=alpha),
        grid=(rows // tm,),
        in_specs=[tok(0), pl.BlockSpec((tm, TOP_K), lambda i: (tile0 + i, 0))]
                 + [tok(c * (total_rows // tm)) for c in range(TOP_K)]
                 + [_whole((1, d)), _whole((1, d))],
        out_specs=pl.BlockSpec((tm, d), lambda i: (i, 0)),
        out_shape=jax.ShapeDtypeStruct((rows, d), F32),
        compiler_params=_params(("parallel",)),
        name="ln_out",
    )(x1r, gates, *([y_rows] * TOP_K), g, b)


def _route(top_idx, n_exp, bm):
    t = top_idx.shape[0]
    tk = t * TOP_K
    assert n_exp * tk <= 2 ** 31
    packed = jnp.sort(top_idx.reshape(-1) * tk + jnp.arange(tk, dtype=jnp.int32))
    sorted_e, order = packed // tk, packed % tk
    experts = jnp.arange(n_exp, dtype=jnp.int32)
    starts = jnp.sum(sorted_e[None, :] < experts[:, None], axis=1, dtype=jnp.int32)
    counts = jnp.sum(sorted_e[None, :] == experts[:, None], axis=1, dtype=jnp.int32)
    padded = ((counts + bm - 1) // bm) * bm
    pends = jnp.cumsum(padded)
    pstarts = pends - padded
    n_blocks = (tk + n_exp * bm) // bm
    blk_start = jnp.arange(n_blocks, dtype=jnp.int32) * bm
    blk_e = jnp.minimum(jnp.sum(pends[None, :] <= blk_start[:, None], axis=1, dtype=jnp.int32),
                        n_exp - 1)
    n_valid = (pends[-1:] // bm).astype(jnp.int32)
    lane = jnp.arange(bm, dtype=jnp.int32)[None, :]
    row_in_expert = (blk_start - pstarts[blk_e])[:, None] + lane
    valid = row_in_expert < counts[blk_e][:, None]
    pick = order[jnp.clip(starts[blk_e][:, None] + row_in_expert, 0, tk - 1)]
    tok, choice = pick // TOP_K, pick % TOP_K
    src = jnp.where(valid, tok, 0)
    spare = tk + blk_start[:, None] + lane - (starts + counts)[blk_e][:, None]
    dst = jnp.where(valid, choice * t + tok, spare)
    return src[:, None, :], dst[:, None, :], blk_e, n_valid


def _channel_dft_table():
    n = np.arange(HEAD)
    ang = 2.0 * np.pi * ((n[:, None] * n[None, :]) % HEAD) / HEAD
    return jnp.asarray(np.concatenate([np.cos(ang), np.sin(ang)], axis=1), dtype=BF16)


def kernel(x_prompt, x_sample, ln_in_g, ln_in_b, w_in, sgu_norm_g, sgu_norm_b, sgu_w, sgu_b,
           w_o, ln1_g, ln1_b, router_w, router_b, w_gu, b_gu, w_down, b_down, ln2_g, ln2_b):
    depth, d, d_in = w_in.shape
    assert depth == 1, "single-layer encoder only"
    n_heads_b, chunk, _ = sgu_w.shape[1:]
    assert chunk == HEAD
    d_b = n_heads_b * HEAD
    d_a = d_in - 2 * d_b
    n_exp = router_w.shape[2]
    d_ff = w_down.shape[2]
    alpha = (2.0 * depth) ** 0.25
    vec = lambda p: p.reshape(1, -1)

    w_in_b = w_in[0].astype(BF16)
    w_o_b = w_o[0].astype(BF16)
    sgu_w_b = sgu_w[0].astype(BF16)
    sgu_bias = jnp.broadcast_to(sgu_b[0][:, :, None], (n_heads_b, HEAD, HEAD))
    cs = _channel_dft_table()
    rw_hi = router_w[0].astype(BF16)
    rw_lo = (router_w[0] - rw_hi.astype(F32)).astype(BF16)
    w_g, w_l = _split_gu(w_gu[0].reshape(n_exp * d, 2 * d_ff))
    w_g = w_g.reshape(n_exp, d, d_ff)
    w_l = w_l.reshape(n_exp, d, d_ff)
    b_g = b_gu[0][:, None, 0::2]
    b_l = b_gu[0][:, None, 1::2]
    w_d = _cast_bf16(w_down[0].reshape(n_exp * d_ff, d)).reshape(n_exp, d_ff, d)
    b_d = b_down[0][:, None, :]

    requests = (x_prompt, x_sample)
    total_rows = sum(x.shape[0] * x.shape[1] for x in requests)
    xs, ya_los, ya_his, ybs = [], [], [], []
    for x in requests:
        n_batch, seq, _ = x.shape
        x2d = x.reshape(n_batch * seq, d)
        a, q, yb = _mix_in(x2d, vec(ln_in_g), vec(ln_in_b), w_in_b, cs, vec(sgu_norm_g[0]),
                           vec(sgu_norm_b[0]), sgu_w_b, sgu_bias, d_a, d_b)
        ya_lo, ya_hi = _seq_dft(a.reshape(n_batch, seq, d_a), q.reshape(n_batch, seq, d_a))
        xs.append(x2d)
        ya_los.append(ya_lo)
        ya_his.append(ya_hi)
        ybs.append(yb)
    x1r, top_idx, top_gates = _mix_out(xs, ya_los, ya_his, ybs, vec(ln_in_g), vec(ln_in_b),
                                       w_o_b, vec(ln1_g[0]), vec(ln1_b[0]), rw_hi, rw_lo,
                                       vec(router_b[0]), alpha)

    bm = min(MOE_ROWS, total_rows * TOP_K)
    gates = top_gates.T
    src, dst, blk_e, n_valid = _route(top_idx.T, n_exp, bm)
    y_rows = _moe(blk_e, n_valid, src, dst, x1r, w_g, w_l, b_g, b_l, w_d, b_d,
                  total_rows * TOP_K + n_exp * bm)

    outs, row0 = [], 0
    for x in requests:
        rows = x.shape[0] * x.shape[1]
        out = _ln_out(x1r, y_rows, gates, vec(ln2_g[0]), vec(ln2_b[0]), alpha,
                      rows, row0, total_rows)
        outs.append(out.reshape(x.shape))
        row0 += rows
    return tuple(outs)
```

```python
import functools
import math

import jax
import jax.numpy as jnp
import numpy as np
from jax import lax
from jax.experimental import pallas as pl
from jax.experimental.pallas import tpu as pltpu

F32 = jnp.float32
BF16 = jnp.bfloat16

LN_EPS = 1e-5
TOP_K = 4
HEAD = 128
LANES = 128
SWIGLU_LIMIT = 7.0
SWIGLU_ALPHA = 1.702
VMEM_LIMIT_BYTES = 56 * 1024 * 1024

ROW_TILE = 256
MIX_IN_ROWS = 512
DFT_K_TILE = 512
DFT_S_TILE = 1024
DFT_ACC_BYTES = 9 * 1024 * 1024
MOE_ROWS = 512
MOE_FF_TILE = 512
MOE_OUT_TILE = 512
MOE_ROW_PAD = 8
PREP_ROWS = 512
SPLIT_GROUP = 512
PREP_STREAMS = 4


def _layer_norm(x, g, b):
    mu = jnp.mean(x, axis=-1, keepdims=True)
    xc = x - mu
    var = jnp.mean(xc * xc, axis=-1, keepdims=True)
    return xc * lax.rsqrt(var + LN_EPS) * g + b


def _gelu(z):
    return 0.5 * z * (1.0 + lax.erf(z * (1.0 / math.sqrt(2.0))))


def _dot(a, b):
    return jnp.dot(a, b, preferred_element_type=F32)


def _params(semantics):
    return pltpu.CompilerParams(dimension_semantics=semantics,
                                vmem_limit_bytes=VMEM_LIMIT_BYTES)


def _whole(shape):
    return pl.BlockSpec(shape, lambda *_: (0,) * len(shape),
                        pipeline_mode=pl.Buffered(1))


def _store_token_rows(ref, val):
    rpt = val.shape[1] // LANES
    for c in range(rpt):
        ref[pl.ds(c, val.shape[0], stride=rpt), :] = val[:, c * LANES:(c + 1) * LANES]


def _load_token_rows(ref, n_tokens, rpt, pitch=None):
    return [ref[pl.ds(c, n_tokens, stride=pitch or rpt), :] for c in range(rpt)]


def _mix_in_body(x_ref, g_ref, b_ref, win_ref, cs_ref, sg_ref, sb_ref, sw_ref, sbias_ref,
                 a_ref, q_ref, yb_ref, *, d_a, d_b):
    xb = _layer_norm(x_ref[...], g_ref[...], b_ref[...]).astype(BF16)
    rows = xb.shape[0]

    za = _dot(xb, win_ref[:, :d_a]).astype(BF16)
    for h in range(d_a // HEAD):
        hs = slice(h * HEAD, (h + 1) * HEAD)
        ab = _dot(za[:, hs], cs_ref[...])
        a_ref[:, hs] = ab[:, :HEAD].astype(BF16)
        q_ref[:, hs] = ab[:, HEAD:].astype(BF16)

    u = _gelu(_dot(xb, win_ref[:, d_a:d_a + d_b]))
    v = _gelu(_dot(xb, win_ref[:, d_a + d_b:]))
    for h in range(d_b // HEAD):
        hs = slice(h * HEAD, (h + 1) * HEAD)
        vn = _layer_norm(v[:, hs], sg_ref[:, hs], sb_ref[:, hs]).astype(BF16)
        for c in range(rows // HEAD):
            rs = slice(c * HEAD, (c + 1) * HEAD)
            gate = _dot(sw_ref[h], vn[rs, :]) + sbias_ref[h]
            yb_ref[rs, hs] = (u[rs, hs] * gate).astype(BF16)


def _mix_in(x2d, ln_g, ln_b, w_in, cs, sgu_g, sgu_b, sgu_w, sgu_bias, d_a, d_b):
    rows, d = x2d.shape
    tm = min(MIX_IN_ROWS, rows)
    n_heads_b = d_b // HEAD
    row = lambda w: pl.BlockSpec((tm, w), lambda i: (i, 0))
    return pl.pallas_call(
        functools.partial(_mix_in_body, d_a=d_a, d_b=d_b),
        grid=(rows // tm,),
        in_specs=[row(d), _whole((1, d)), _whole((1, d)), _whole(w_in.shape),
                  _whole(cs.shape), _whole((1, d_b)), _whole((1, d_b)),
                  _whole((n_heads_b, HEAD, HEAD)), _whole((n_heads_b, HEAD, HEAD))],
        out_specs=[row(d_a), row(d_a), row(d_b)],
        out_shape=[jax.ShapeDtypeStruct((rows, d_a), BF16),
                   jax.ShapeDtypeStruct((rows, d_a), BF16),
                   jax.ShapeDtypeStruct((rows, d_b), BF16)],
        compiler_params=_params(("parallel",)),
        name="mix_in",
    )(x2d, ln_g, ln_b, w_in, cs, sgu_g, sgu_b, sgu_w, sgu_bias)


def _seq_dft_body(a_ref, q_ref, c0_ref, s0_ref, perm_ref, lo_ref, hi_ref,
                  accp_ref, accz_ref, cb_ref, sb_ref, ca_ref, sa_ref, cd_ref, sd_ref,
                  *, seq, scale):
    i, j = pl.program_id(1), pl.program_id(2)
    n_batch, tk, _ = lo_ref.shape
    tkx, ts = cb_ref.shape
    w = 2.0 * math.pi / seq

    @pl.when(j == 0)
    def _():
        accp_ref[...] = jnp.zeros_like(accp_ref)
        accz_ref[...] = jnp.zeros_like(accz_ref)
        ds = lax.broadcasted_iota(jnp.int32, (1, ts), 1)
        ph = (((i * tk) * ds) & (seq - 1)).astype(F32) * w
        ci, si = jnp.cos(ph), jnp.sin(ph)
        cb_ref[...] = ci * c0_ref[...] - si * s0_ref[...]
        sb_ref[...] = si * c0_ref[...] + ci * s0_ref[...]
        k = i * tk + lax.broadcasted_iota(jnp.int32, (tkx, HEAD), 0)
        pd = ((k * ts) & (seq - 1)).astype(F32) * w
        cd_ref[...] = jnp.cos(pd)
        sd_ref[...] = jnp.sin(pd)
        ca_ref[...] = jnp.ones_like(ca_ref)
        sa_ref[...] = jnp.zeros_like(sa_ref)

    ca, sa = ca_ref[...], sa_ref[...]
    cos_parts, sin_parts = [], []
    for g in range(ts // HEAD):
        gs = slice(g * HEAD, (g + 1) * HEAD)
        cb, sb = cb_ref[:, gs], sb_ref[:, gs]
        cos_parts.append((ca * cb - sa * sb).astype(BF16))
        sin_parts.append((sa * cb + ca * sb).astype(BF16))
    cos_t = jnp.concatenate(cos_parts, axis=1)
    sin_t = jnp.concatenate(sin_parts, axis=1)
    for b in range(n_batch):
        accp_ref[b] += _dot(cos_t, a_ref[b])
        accz_ref[b] += _dot(sin_t, q_ref[b])
    cd, sd = cd_ref[...], sd_ref[...]
    ca_ref[...] = ca * cd - sa * sd
    sa_ref[...] = sa * cd + ca * sd

    @pl.when(j == pl.num_programs(2) - 1)
    def _():
        first_row = lax.broadcasted_iota(jnp.int32, (tk, 1), 0) == 0
        for b in range(n_batch):
            p, z = accp_ref[b], accz_ref[b]
            lo_ref[b] = ((p[:tk] - z[:tk]) * scale).astype(BF16)
            mirror = ((p + z) * scale).astype(BF16)
            rev = _dot(perm_ref[...], mirror[:tk]).astype(BF16)
            hi_ref[b] = jnp.where(first_row, mirror[tk:tk + 1], rev)


def _seq_dft(a3d, q3d):
    n_batch, seq, d_a = a3d.shape
    assert seq & (seq - 1) == 0, "sequence length must be a power of two"
    half = seq // 2
    tk, ts = min(DFT_K_TILE, half), min(DFT_S_TILE, seq)
    tkx = tk + 8
    tc = d_a
    while tc > LANES and 2 * n_batch * tkx * tc * 4 > DFT_ACC_BYTES:
        tc //= 2
    n_i = half // tk
    scale = 1.0 / math.sqrt(seq * HEAD)
    ang = 2.0 * np.pi * np.outer(np.arange(tkx), np.arange(ts)) / seq
    c0 = jnp.asarray(np.cos(ang), dtype=F32)
    s0 = jnp.asarray(np.sin(ang), dtype=F32)
    perm = np.zeros((tk, tk), np.float32)
    r = np.arange(1, tk)
    perm[r, tk - r] = 1.0
    in_spec = pl.BlockSpec((n_batch, ts, tc), lambda c, i, j: (0, j, c))
    out = jax.ShapeDtypeStruct((n_batch, half, d_a), BF16)
    return pl.pallas_call(
        functools.partial(_seq_dft_body, seq=seq, scale=scale),
        grid=(d_a // tc, n_i, seq // ts),
        in_specs=[in_spec, in_spec, _whole((tkx, ts)), _whole((tkx, ts)), _whole((tk, tk))],
        out_specs=[pl.BlockSpec((n_batch, tk, tc), lambda c, i, j: (0, i, c)),
                   pl.BlockSpec((n_batch, tk, tc), lambda c, i, j: (0, n_i - 1 - i, c))],
        out_shape=[out, out],
        scratch_shapes=[pltpu.VMEM((n_batch, tkx, tc), F32), pltpu.VMEM((n_batch, tkx, tc), F32),
                        pltpu.VMEM((tkx, ts), F32), pltpu.VMEM((tkx, ts), F32)]
                       + [pltpu.VMEM((tkx, HEAD), F32)] * 4,
        compiler_params=_params(("parallel", "parallel", "arbitrary")),
        name="seq_dft",
    )(a3d, q3d, c0, s0, jnp.asarray(perm, dtype=BF16))


def _top_k_gates(logits_t):
    n_exp, n_tok = logits_t.shape
    expert = lax.broadcasted_iota(jnp.int32, (n_exp, n_tok), 0)
    slot = lax.broadcasted_iota(jnp.int32, (TOP_K, n_tok), 0)
    work = logits_t
    top = jnp.zeros((TOP_K, n_tok), F32)
    idx = jnp.zeros((TOP_K, n_tok), jnp.int32)
    for c in range(TOP_K):
        best = jnp.max(work, axis=0, keepdims=True)
        where = jnp.min(jnp.where(work == best, expert, n_exp), axis=0, keepdims=True)
        top = jnp.where(slot == c, best, top)
        idx = jnp.where(slot == c, where, idx)
        work = jnp.where(expert == where, -jnp.inf, work)
    e = jnp.exp(top - jnp.max(top, axis=0, keepdims=True))
    return idx, e / jnp.sum(e, axis=0, keepdims=True)


def _mix_out_body(*refs, alpha, d_a, seq_tiles):
    n_req = len(seq_tiles)
    per_req = [refs[4 * q:4 * q + 4] for q in range(n_req)]
    (g0_ref, b0_ref, wo_ref, g1_ref, b1_ref, rwh_ref, rwl_ref, rb_ref,
     x1r_ref, idx_ref, gate_ref) = refs[4 * n_req:]
    req, tile = pl.program_id(0), pl.program_id(1)

    def tile_of_request(x_ref, lo_ref, hi_ref, yb_ref, n_p):
        xln = _layer_norm(x_ref[...], g0_ref[...], b0_ref[...])
        ya = jnp.where(tile % n_p < n_p // 2, lo_ref[...], hi_ref[...])
        mix = _dot(ya, wo_ref[:d_a, :]) + _dot(yb_ref[...], wo_ref[d_a:, :])
        x1 = _layer_norm(alpha * xln + mix, g1_ref[...], b1_ref[...])
        _store_token_rows(x1r_ref, x1)
        hi = x1.astype(BF16)
        lo = (x1 - hi.astype(F32)).astype(BF16)
        logits = (_dot(hi, rwh_ref[...]) + _dot(hi, rwl_ref[...])
                  + _dot(lo, rwh_ref[...]) + rb_ref[...])
        idx_ref[...], gate_ref[...] = _top_k_gates(logits.T)

    for q in range(n_req):
        pl.when(req == q)(functools.partial(tile_of_request, *per_req[q], seq_tiles[q]))


def _mix_out(xs, ya_los, ya_his, ybs, g0, b0, w_o, g1, b1, rw_hi, rw_lo, rb, alpha):
    n_req = len(xs)
    rows, d = xs[0].shape
    assert all(x.shape == (rows, d) for x in xs)
    d_a = ya_los[0].shape[2]
    n_exp = rw_hi.shape[1]
    tm = min([ROW_TILE] + [lo.shape[1] for lo in ya_los])
    tiles = rows // tm
    seq_tiles = tuple(2 * lo.shape[1] // tm for lo in ya_los)
    rpt = d // LANES

    def tile_of(q, req, tile):
        return jnp.where(req == q, tile, jnp.where(req < q, 0, tiles - 1))

    args, in_specs = [], []
    for q in range(n_req):
        n_p = seq_tiles[q]
        rows_spec = lambda w, q=q: pl.BlockSpec(
            (tm, w), lambda req, tile: (tile_of(q, req, tile), 0))
        lo_spec = pl.BlockSpec(
            (None, tm, d_a), lambda req, tile, q=q, n_p=n_p: (
                tile_of(q, req, tile) // n_p,
                jnp.minimum(tile_of(q, req, tile) % n_p, n_p // 2 - 1), 0))
        hi_spec = pl.BlockSpec(
            (None, tm, d_a), lambda req, tile, q=q, n_p=n_p: (
                tile_of(q, req, tile) // n_p,
                jnp.maximum(tile_of(q, req, tile) % n_p - n_p // 2, 0), 0))
        args += [xs[q], ya_los[q], ya_his[q], ybs[q]]
        in_specs += [rows_spec(d), lo_spec, hi_spec, rows_spec(ybs[q].shape[1])]
    vec = _whole((1, d))
    col = pl.BlockSpec((TOP_K, tm), lambda req, tile: (0, req * tiles + tile))
    return pl.pallas_call(
        functools.partial(_mix_out_body, alpha=alpha, d_a=d_a, seq_tiles=seq_tiles),
        grid=(n_req, tiles),
        in_specs=in_specs + [vec, vec, _whole(w_o.shape), vec, vec, _whole(rw_hi.shape),
                             _whole(rw_lo.shape), _whole((1, n_exp))],
        out_specs=[pl.BlockSpec((tm * rpt, LANES), lambda req, tile: (req * tiles + tile, 0)),
                   col, col],
        out_shape=[jax.ShapeDtypeStruct((n_req * rows * rpt, LANES), F32),
                   jax.ShapeDtypeStruct((TOP_K, n_req * rows), jnp.int32),
                   jax.ShapeDtypeStruct((TOP_K, n_req * rows), F32)],
        compiler_params=_params(("parallel", "parallel")),
        name="mix_out",
    )(*args, g0, b0, w_o, g1, b1, rw_hi, rw_lo, rb)


def _column_streams(w2d, tr):
    rows, cols = w2d.shape
    n = PREP_STREAMS if cols % (PREP_STREAMS * SPLIT_GROUP) == 0 else 1
    specs = [pl.BlockSpec((tr, cols // n), functools.partial(lambda i, q: (i, q), q=q))
             for q in range(n)]
    return [w2d] * n, specs


def _split_gu_body(*refs):
    *w_refs, perm_ref, g_ref, l_ref = refs
    half = SPLIT_GROUP // 2
    groups = w_refs[0].shape[1] // SPLIT_GROUP
    for q, w_ref in enumerate(w_refs):
        for c in range(groups):
            w = w_ref[:, c * SPLIT_GROUP:(c + 1) * SPLIT_GROUP].astype(BF16)
            r = _dot(w, perm_ref[...])
            o = (q * groups + c) * half
            g_ref[:, o:o + half] = r[:, :half].astype(BF16)
            l_ref[:, o:o + half] = r[:, half:].astype(BF16)


def _split_gu(w2d):
    rows, cols = w2d.shape
    assert cols % SPLIT_GROUP == 0
    tr = min(PREP_ROWS, rows)
    perm = np.zeros((SPLIT_GROUP, SPLIT_GROUP), np.float32)
    j = np.arange(SPLIT_GROUP // 2)
    perm[2 * j, j] = 1.0
    perm[2 * j + 1, SPLIT_GROUP // 2 + j] = 1.0
    streams, specs = _column_streams(w2d, tr)
    out = jax.ShapeDtypeStruct((rows, cols // 2), BF16)
    return pl.pallas_call(
        _split_gu_body,
        grid=(rows // tr,),
        in_specs=specs + [_whole(perm.shape)],
        out_specs=[pl.BlockSpec((tr, cols // 2), lambda i: (i, 0))] * 2,
        out_shape=[out, out],
        compiler_params=_params(("parallel",)),
        name="split_gu",
    )(*streams, jnp.asarray(perm, dtype=BF16))


def _cast_body(*refs):
    *w_refs, o_ref = refs
    width = w_refs[0].shape[1]
    for q, w_ref in enumerate(w_refs):
        o_ref[:, q * width:(q + 1) * width] = w_ref[...].astype(BF16)


def _cast_bf16(w2d):
    rows, cols = w2d.shape
    tr = min(PREP_ROWS, rows)
    streams, specs = _column_streams(w2d, tr)
    return pl.pallas_call(
        _cast_body,
        grid=(rows // tr,),
        in_specs=specs,
        out_specs=pl.BlockSpec((tr, cols), lambda i: (i, 0)),
        out_shape=jax.ShapeDtypeStruct((rows, cols), BF16),
        compiler_params=_params(("parallel",)),
        name="cast_bf16",
    )(*streams)


def _moe_body(be_ref, nv_ref, src_ref, src_next_ref, dst_ref, dst_prev_ref, x_hbm, wg_hbm,
              wl_hbm, bg_ref, bl_ref, wd_hbm, bd_ref, y_hbm, xbuf_ref, xs_ref, act_ref, obuf_ref,
              wg_ref, wl_ref, wd_ref, gsem, ssem, wsem):
    i, n_valid = pl.program_id(0), nv_ref[0]
    e = be_ref[i]
    e_next = be_ref[jnp.minimum(i + 1, pl.num_programs(0) - 1)]
    first_of_expert = jnp.logical_or(i == 0, be_ref[jnp.maximum(i - 1, 0)] != e)
    last_of_expert = jnp.logical_and(i + 1 < n_valid, e_next != e)

    def up_copies(expert):
        return (pltpu.make_async_copy(wg_hbm.at[expert], wg_ref, wsem.at[0]),
                pltpu.make_async_copy(wl_hbm.at[expert], wl_ref, wsem.at[1]))

    def down_copy(expert):
        return pltpu.make_async_copy(wd_hbm.at[expert], wd_ref, wsem.at[2])

    bm, d = xs_ref.shape
    rpt = d // LANES
    pitch = obuf_ref.shape[0] // bm
    slot = i % 2

    def gather_copy(idx_ref, r, to_slot):
        tok = idx_ref[0, 0, r]
        return pltpu.make_async_copy(x_hbm.at[pl.ds(pl.multiple_of(tok * rpt, rpt), rpt), :],
                                     xbuf_ref.at[to_slot, pl.ds(r * pitch, rpt), :], gsem)

    def scatter_copy(idx_ref, r):
        row = idx_ref[0, 0, r]
        return pltpu.make_async_copy(obuf_ref.at[pl.ds(r * pitch, rpt), :],
                                     y_hbm.at[pl.ds(pl.multiple_of(row * rpt, rpt), rpt), :], ssem)

    def wait_gather(of_slot):
        pltpu.make_async_copy(x_hbm.at[pl.ds(0, bm * rpt), :],
                              xbuf_ref.at[of_slot, pl.ds(0, bm * rpt), :], gsem).wait()

    def wait_scatter():
        pltpu.make_async_copy(obuf_ref.at[pl.ds(0, bm * rpt), :],
                              y_hbm.at[pl.ds(0, bm * rpt), :], ssem).wait()

    @pl.when(i < n_valid)
    def _():
        @pl.when(i == 0)
        def _():
            obuf_ref[...] = jnp.zeros_like(obuf_ref)
            for r in range(bm):
                gather_copy(src_ref, r, 0).start(priority=r % 2)
            for copy in up_copies(e):
                copy.start()

        @pl.when(first_of_expert)
        def _():
            down_copy(e).start()
            for copy in up_copies(e):
                copy.wait()

        wait_gather(slot)
        for c, part in enumerate(_load_token_rows(xbuf_ref.at[slot], bm, rpt, pitch)):
            xs_ref[:, c * LANES:(c + 1) * LANES] = part.astype(BF16)
        x = xs_ref[...]
        d_ff = wg_ref.shape[1]
        tf = min(MOE_FF_TILE, d_ff)
        n_ff = d_ff // tf
        for f in range(n_ff):
            fs = slice(f * tf, (f + 1) * tf)
            glu = jnp.minimum(_dot(x, wg_ref[:, fs]) + bg_ref[0, :, fs], SWIGLU_LIMIT)
            lin = jnp.clip(_dot(x, wl_ref[:, fs]) + bl_ref[0, :, fs],
                           -SWIGLU_LIMIT, SWIGLU_LIMIT)
            for r in range(f * bm // n_ff, (f + 1) * bm // n_ff):
                gather_copy(src_next_ref, r, 1 - slot).start(priority=r % 2)
                scatter_copy(dst_prev_ref, r).start(priority=r % 2)
            act = glu * (1.0 / (1.0 + jnp.exp(-SWIGLU_ALPHA * glu))) * (lin + 1.0)
            act_ref[:, fs] = act.astype(BF16)

        wait_scatter()

        @pl.when(last_of_expert)
        def _():
            for copy in up_copies(e_next):
                copy.start()

        @pl.when(first_of_expert)
        def _():
            down_copy(e).wait()

        tn = min(MOE_OUT_TILE, d)
        for n in range(d // tn):
            ns = slice(n * tn, (n + 1) * tn)
            done = _dot(act_ref[...], wd_ref[:, ns]) + bd_ref[0, :, ns]
            for c in range(tn // LANES):
                obuf_ref[pl.ds(n * (tn // LANES) + c, bm, stride=pitch), :] = (
                    done[:, c * LANES:(c + 1) * LANES])

        @pl.when(i == n_valid - 1)
        def _():
            for r in range(bm):
                scatter_copy(dst_ref, r).start(priority=r % 2)
            wait_gather(1 - slot)
            wait_scatter()


def _moe(blk_e, n_valid, src, dst, x1r, w_g, w_l, b_g, b_l, w_d, b_d, n_out_rows):
    n_blocks, _, bm = src.shape
    n_exp, d, d_ff = w_g.shape
    rpt = d // LANES
    pitch = rpt + MOE_ROW_PAD if rpt % 16 == 0 else rpt
    expert = lambda i, be, nv: (be[i], 0, 0)
    idx_block = lambda index_map: pl.BlockSpec((1, 1, bm), index_map, memory_space=pltpu.SMEM)
    grid_spec = pltpu.PrefetchScalarGridSpec(
        num_scalar_prefetch=2,
        grid=(n_blocks,),
        in_specs=[
            idx_block(lambda i, be, nv: (i, 0, 0)),
            idx_block(lambda i, be, nv: (jnp.minimum(i + 1, n_blocks - 1), 0, 0)),
            idx_block(lambda i, be, nv: (i, 0, 0)),
            idx_block(lambda i, be, nv: (jnp.maximum(i - 1, 0), 0, 0)),
            pl.BlockSpec(memory_space=pl.ANY),
            pl.BlockSpec(memory_space=pl.ANY),
            pl.BlockSpec(memory_space=pl.ANY),
            pl.BlockSpec((1, 1, d_ff), expert),
            pl.BlockSpec((1, 1, d_ff), expert),
            pl.BlockSpec(memory_space=pl.ANY),
            pl.BlockSpec((1, 1, d), expert),
        ],
        out_specs=pl.BlockSpec(memory_space=pl.ANY),
        scratch_shapes=[pltpu.VMEM((2, bm * pitch, LANES), F32), pltpu.VMEM((bm, d), BF16),
                        pltpu.VMEM((bm, d_ff), BF16), pltpu.VMEM((bm * pitch, LANES), F32),
                        pltpu.VMEM((d, d_ff), BF16), pltpu.VMEM((d, d_ff), BF16),
                        pltpu.VMEM((d_ff, d), BF16),
                        pltpu.SemaphoreType.DMA(()), pltpu.SemaphoreType.DMA(()),
                        pltpu.SemaphoreType.DMA((3,))],
    )
    return pl.pallas_call(
        _moe_body,
        grid_spec=grid_spec,
        out_shape=jax.ShapeDtypeStruct((n_out_rows * rpt, LANES), F32),
        compiler_params=_params(("arbitrary",)),
        name="moe",
    )(blk_e, n_valid, src, src, dst, dst, x1r, w_g, w_l, b_g, b_l, w_d, b_d)


def _ln_out_body(x1r_ref, gate_ref, *refs, alpha):
    *y_refs, g_ref, b_ref, o_ref = refs
    tm, d = o_ref.shape
    rpt = d // LANES
    gates = gate_ref[...]
    parts = [alpha * p for p in _load_token_rows(x1r_ref, tm, rpt)]
    for c, y_ref in enumerate(y_refs):
        gate = jnp.broadcast_to(gates[:, c:c + 1], (tm, LANES))
        parts = [p + gate * y for p, y in zip(parts, _load_token_rows(y_ref, tm, rpt))]
    o_ref[...] = _layer_norm(jnp.concatenate(parts, axis=1), g_ref[...], b_ref[...])


def _ln_out(x1r, y_rows, gates, g, b, alpha, rows, row0, total_rows):
    d = g.shape[1]
    rpt = d // LANES
    tm = min(ROW_TILE, rows)
    assert row0 % tm == 0 and total_rows % tm == 0
    tile0 = row0 // tm
    tok = lambda off: pl.BlockSpec((tm * rpt, LANES), lambda i: (off + tile0 + i, 0))
    return pl.pallas_call(
        functools.partial(_ln_out_body, alpha=alpha),
        grid=(rows // tm,),
        in_specs=[tok(0), pl.BlockSpec((tm, TOP_K), lambda i: (tile0 + i, 0))]
                 + [tok(c * (total_rows // tm)) for c in range(TOP_K)]
                 + [_whole((1, d)), _whole((1, d))],
        out_specs=pl.BlockSpec((tm, d), lambda i: (i, 0)),
        out_shape=jax.ShapeDtypeStruct((rows, d), F32),
        compiler_params=_params(("parallel",)),
        name="ln_out",
    )(x1r, gates, *([y_rows] * TOP_K), g, b)


def _route(top_idx, n_exp, bm):
    t = top_idx.shape[0]
    tk = t * TOP_K
    assert n_exp * tk <= 2 ** 31
    packed = jnp.sort(top_idx.reshape(-1) * tk + jnp.arange(tk, dtype=jnp.int32))
    sorted_e, order = packed // tk, packed % tk
    experts = jnp.arange(n_exp, dtype=jnp.int32)
    starts = jnp.sum(sorted_e[None, :] < experts[:, None], axis=1, dtype=jnp.int32)
    counts = jnp.sum(sorted_e[None, :] == experts[:, None], axis=1, dtype=jnp.int32)
    padded = ((counts + bm - 1) // bm) * bm
    pends = jnp.cumsum(padded)
    pstarts = pends - padded
    n_blocks = (tk + n_exp * bm) // bm
    blk_start = jnp.arange(n_blocks, dtype=jnp.int32) * bm
    blk_e = jnp.minimum(jnp.sum(pends[None, :] <= blk_start[:, None], axis=1, dtype=jnp.int32),
                        n_exp - 1)
    n_valid = (pends[-1:] // bm).astype(jnp.int32)
    lane = jnp.arange(bm, dtype=jnp.int32)[None, :]
    row_in_expert = (blk_start - pstarts[blk_e])[:, None] + lane
    valid = row_in_expert < counts[blk_e][:, None]
    pick = order[jnp.clip(starts[blk_e][:, None] + row_in_expert, 0, tk - 1)]
    tok, choice = pick // TOP_K, pick % TOP_K
    src = jnp.where(valid, tok, 0)
    spare = tk + blk_start[:, None] + lane - (starts + counts)[blk_e][:, None]
    dst = jnp.where(valid, choice * t + tok, spare)
    return src[:, None, :], dst[:, None, :], blk_e, n_valid


def _channel_dft_table():
    n = np.arange(HEAD)
    ang = 2.0 * np.pi * ((n[:, None] * n[None, :]) % HEAD) / HEAD
    return jnp.asarray(np.concatenate([np.cos(ang), np.sin(ang)], axis=1), dtype=BF16)


def kernel(x_prompt, x_sample, ln_in_g, ln_in_b, w_in, sgu_norm_g, sgu_norm_b, sgu_w, sgu_b,
           w_o, ln1_g, ln1_b, router_w, router_b, w_gu, b_gu, w_down, b_down, ln2_g, ln2_b):
    depth, d, d_in = w_in.shape
    assert depth == 1, "single-layer encoder only"
    n_heads_b, chunk, _ = sgu_w.shape[1:]
    assert chunk == HEAD
    d_b = n_heads_b * HEAD
    d_a = d_in - 2 * d_b
    n_exp = router_w.shape[2]
    d_ff = w_down.shape[2]
    alpha = (2.0 * depth) ** 0.25
    vec = lambda p: p.reshape(1, -1)

    w_in_b = w_in[0].astype(BF16)
    w_o_b = w_o[0].astype(BF16)
    sgu_w_b = sgu_w[0].astype(BF16)
    sgu_bias = jnp.broadcast_to(sgu_b[0][:, :, None], (n_heads_b, HEAD, HEAD))
    cs = _channel_dft_table()
    rw_hi = router_w[0].astype(BF16)
    rw_lo = (router_w[0] - rw_hi.astype(F32)).astype(BF16)
    w_g, w_l = _split_gu(w_gu[0].reshape(n_exp * d, 2 * d_ff))
    w_g = w_g.reshape(n_exp, d, d_ff)
    w_l = w_l.reshape(n_exp, d, d_ff)
    b_g = b_gu[0][:, None, 0::2]
    b_l = b_gu[0][:, None, 1::2]
    w_d = _cast_bf16(w_down[0].reshape(n_exp * d_ff, d)).reshape(n_exp, d_ff, d)
    b_d = b_down[0][:, None, :]

    requests = (x_prompt, x_sample)
    total_rows = sum(x.shape[0] * x.shape[1] for x in requests)
    xs, ya_los, ya_his, ybs = [], [], [], []
    for x in requests:
        n_batch, seq, _ = x.shape
        x2d = x.reshape(n_batch * seq, d)
        a, q, yb = _mix_in(x2d, vec(ln_in_g), vec(ln_in_b), w_in_b, cs, vec(sgu_norm_g[0]),
                           vec(sgu_norm_b[0]), sgu_w_b, sgu_bias, d_a, d_b)
        ya_lo, ya_hi = _seq_dft(a.reshape(n_batch, seq, d_a), q.reshape(n_batch, seq, d_a))
        xs.append(x2d)
        ya_los.append(ya_lo)
        ya_his.append(ya_hi)
        ybs.append(yb)
    x1r, top_idx, top_gates = _mix_out(xs, ya_los, ya_his, ybs, vec(ln_in_g), vec(ln_in_b),
                                       w_o_b, vec(ln1_g[0]), vec(ln1_b[0]), rw_hi, rw_lo,
                                       vec(router_b[0]), alpha)

    bm = min(MOE_ROWS, total_rows * TOP_K)
    gates = top_gates.T
    src, dst, blk_e, n_valid = _route(top_idx.T, n_exp, bm)
    y_rows = _moe(blk_e, n_valid, src, dst, x1r, w_g, w_l, b_g, b_l, w_d, b_d,
                  total_rows * TOP_K + n_exp * bm)

    outs, row0 = [], 0
    for x in requests:
        rows = x.shape[0] * x.shape[1]
        out = _ln_out(x1r, y_rows, gates, vec(ln2_g[0]), vec(ln2_b[0]), alpha,
                      rows, row0, total_rows)
        outs.append(out.reshape(x.shape))
        row0 += rows
    return tuple(outs)
```

```python
import functools
import math

import jax
import jax.numpy as jnp
import numpy as np
from jax import lax
from jax.experimental import pallas as pl
from jax.experimental.pallas import tpu as pltpu

F32 = jnp.float32
BF16 = jnp.bfloat16

LN_EPS = 1e-5
TOP_K = 4
HEAD = 128
LANES = 128
SWIGLU_LIMIT = 7.0
SWIGLU_ALPHA = 1.702
VMEM_LIMIT_BYTES = 56 * 1024 * 1024

ROW_TILE = 256
MIX_IN_ROWS = 512
MIX_IN_COLS = 512
DFT_K_TILE = 512
DFT_S_TILE = 1024
DFT_ACC_BYTES = 9 * 1024 * 1024
MOE_ROWS = 512
MOE_FF_TILE = 512
MOE_OUT_TILE = 512
MOE_ROW_PAD = 8
PREP_ROWS = 512
SPLIT_GROUP = 512
PREP_STREAMS = 4


def _layer_norm(x, g, b):
    mu = jnp.mean(x, axis=-1, keepdims=True)
    xc = x - mu
    var = jnp.mean(xc * xc, axis=-1, keepdims=True)
    return xc * lax.rsqrt(var + LN_EPS) * g + b


def _gelu(z):
    return 0.5 * z * (1.0 + lax.erf(z * (1.0 / math.sqrt(2.0))))


def _dot(a, b):
    return jnp.dot(a, b, preferred_element_type=F32)


def _params(semantics):
    return pltpu.CompilerParams(dimension_semantics=semantics,
                                vmem_limit_bytes=VMEM_LIMIT_BYTES)


def _whole(shape):
    return pl.BlockSpec(shape, lambda *_: (0,) * len(shape),
                        pipeline_mode=pl.Buffered(1))


def _store_token_rows(ref, val):
    rpt = val.shape[1] // LANES
    for c in range(rpt):
        ref[pl.ds(c, val.shape[0], stride=rpt), :] = val[:, c * LANES:(c + 1) * LANES]


def _load_token_rows(ref, n_tokens, rpt, pitch=None):
    return [ref[pl.ds(c, n_tokens, stride=pitch or rpt), :] for c in range(rpt)]


def _mix_in_body(x_ref, g_ref, b_ref, win_ref, cs_ref, sg_ref, sb_ref, sw_ref, sbias_ref,
                 a_ref, q_ref, yb_ref, *, d_a, d_b):
    xb = _layer_norm(x_ref[...], g_ref[...], b_ref[...]).astype(BF16)
    rows = xb.shape[0]

    wa = min(MIX_IN_COLS, d_a)
    for p in range(d_a // wa):
        za = _dot(xb, win_ref[:, p * wa:(p + 1) * wa]).astype(BF16)
        for h in range(wa // HEAD):
            hs = slice(p * wa + h * HEAD, p * wa + (h + 1) * HEAD)
            ab = _dot(za[:, h * HEAD:(h + 1) * HEAD], cs_ref[...])
            a_ref[:, hs] = ab[:, :HEAD].astype(BF16)
            q_ref[:, hs] = ab[:, HEAD:].astype(BF16)

    wb = min(MIX_IN_COLS, d_b)
    for p in range(d_b // wb):
        u = _gelu(_dot(xb, win_ref[:, d_a + p * wb:d_a + (p + 1) * wb]))
        v = _gelu(_dot(xb, win_ref[:, d_a + d_b + p * wb:d_a + d_b + (p + 1) * wb]))
        for hh in range(wb // HEAD):
            h = p * (wb // HEAD) + hh
            ls = slice(hh * HEAD, (hh + 1) * HEAD)
            hs = slice(h * HEAD, (h + 1) * HEAD)
            vn = _layer_norm(v[:, ls], sg_ref[:, hs], sb_ref[:, hs]).astype(BF16)
            for c in range(rows // HEAD):
                rs = slice(c * HEAD, (c + 1) * HEAD)
                gate = _dot(sw_ref[h], vn[rs, :]) + sbias_ref[h]
                yb_ref[rs, hs] = (u[rs, ls] * gate).astype(BF16)


def _mix_in(x2d, ln_g, ln_b, w_in, cs, sgu_g, sgu_b, sgu_w, sgu_bias, d_a, d_b):
    rows, d = x2d.shape
    tm = min(MIX_IN_ROWS, rows)
    n_heads_b = d_b // HEAD
    row = lambda w: pl.BlockSpec((tm, w), lambda i: (i, 0))
    return pl.pallas_call(
        functools.partial(_mix_in_body, d_a=d_a, d_b=d_b),
        grid=(rows // tm,),
        in_specs=[row(d), _whole((1, d)), _whole((1, d)), _whole(w_in.shape),
                  _whole(cs.shape), _whole((1, d_b)), _whole((1, d_b)),
                  _whole((n_heads_b, HEAD, HEAD)), _whole((n_heads_b, HEAD, HEAD))],
        out_specs=[row(d_a), row(d_a), row(d_b)],
        out_shape=[jax.ShapeDtypeStruct((rows, d_a), BF16),
                   jax.ShapeDtypeStruct((rows, d_a), BF16),
                   jax.ShapeDtypeStruct((rows, d_b), BF16)],
        compiler_params=_params(("parallel",)),
        name="mix_in",
    )(x2d, ln_g, ln_b, w_in, cs, sgu_g, sgu_b, sgu_w, sgu_bias)


def _seq_dft_body(a_ref, q_ref, c0_ref, s0_ref, perm_ref, lo_ref, hi_ref,
                  accp_ref, accz_ref, cb_ref, sb_ref, ca_ref, sa_ref, cd_ref, sd_ref,
                  *, seq, scale):
    i, j = pl.program_id(1), pl.program_id(2)
    n_batch, tk, _ = lo_ref.shape
    tkx, ts = cb_ref.shape
    w = 2.0 * math.pi / seq

    @pl.when(j == 0)
    def _():
        accp_ref[...] = jnp.zeros_like(accp_ref)
        accz_ref[...] = jnp.zeros_like(accz_ref)
        ds = lax.broadcasted_iota(jnp.int32, (1, ts), 1)
        ph = (((i * tk) * ds) & (seq - 1)).astype(F32) * w
        ci, si = jnp.cos(ph), jnp.sin(ph)
        cb_ref[...] = ci * c0_ref[...] - si * s0_ref[...]
        sb_ref[...] = si * c0_ref[...] + ci * s0_ref[...]
        k = i * tk + lax.broadcasted_iota(jnp.int32, (tkx, HEAD), 0)
        pd = ((k * ts) & (seq - 1)).astype(F32) * w
        cd_ref[...] = jnp.cos(pd)
        sd_ref[...] = jnp.sin(pd)
        ca_ref[...] = jnp.ones_like(ca_ref)
        sa_ref[...] = jnp.zeros_like(sa_ref)

    ca, sa = ca_ref[...], sa_ref[...]
    cos_parts, sin_parts = [], []
    for g in range(ts // HEAD):
        gs = slice(g * HEAD, (g + 1) * HEAD)
        cb, sb = cb_ref[:, gs], sb_ref[:, gs]
        cos_parts.append((ca * cb - sa * sb).astype(BF16))
        sin_parts.append((sa * cb + ca * sb).astype(BF16))
    cos_t = jnp.concatenate(cos_parts, axis=1)
    sin_t = jnp.concatenate(sin_parts, axis=1)
    for b in range(n_batch):
        accp_ref[b] += _dot(cos_t, a_ref[b])
        accz_ref[b] += _dot(sin_t, q_ref[b])
    cd, sd = cd_ref[...], sd_ref[...]
    ca_ref[...] = ca * cd - sa * sd
    sa_ref[...] = sa * cd + ca * sd

    @pl.when(j == pl.num_programs(2) - 1)
    def _():
        first_row = lax.broadcasted_iota(jnp.int32, (tk, 1), 0) == 0
        for b in range(n_batch):
            p, z = accp_ref[b], accz_ref[b]
            lo_ref[b] = ((p[:tk] - z[:tk]) * scale).astype(BF16)
            mirror = ((p + z) * scale).astype(BF16)
            rev = _dot(perm_ref[...], mirror[:tk]).astype(BF16)
            hi_ref[b] = jnp.where(first_row, mirror[tk:tk + 1], rev)


def _seq_dft(a3d, q3d):
    n_batch, seq, d_a = a3d.shape
    assert seq & (seq - 1) == 0, "sequence length must be a power of two"
    half = seq // 2
    tk, ts = min(DFT_K_TILE, half), min(DFT_S_TILE, seq)
    tkx = tk + 8
    tc = d_a
    while tc > LANES and 2 * n_batch * tkx * tc * 4 > DFT_ACC_BYTES:
        tc //= 2
    n_i = half // tk
    scale = 1.0 / math.sqrt(seq * HEAD)
    ang = 2.0 * np.pi * np.outer(np.arange(tkx), np.arange(ts)) / seq
    c0 = jnp.asarray(np.cos(ang), dtype=F32)
    s0 = jnp.asarray(np.sin(ang), dtype=F32)
    perm = np.zeros((tk, tk), np.float32)
    r = np.arange(1, tk)
    perm[r, tk - r] = 1.0
    in_spec = pl.BlockSpec((n_batch, ts, tc), lambda c, i, j: (0, j, c))
    out = jax.ShapeDtypeStruct((n_batch, half, d_a), BF16)
    return pl.pallas_call(
        functools.partial(_seq_dft_body, seq=seq, scale=scale),
        grid=(d_a // tc, n_i, seq // ts),
        in_specs=[in_spec, in_spec, _whole((tkx, ts)), _whole((tkx, ts)), _whole((tk, tk))],
        out_specs=[pl.BlockSpec((n_batch, tk, tc), lambda c, i, j: (0, i, c)),
                   pl.BlockSpec((n_batch, tk, tc), lambda c, i, j: (0, n_i - 1 - i, c))],
        out_shape=[out, out],
        scratch_shapes=[pltpu.VMEM((n_batch, tkx, tc), F32), pltpu.VMEM((n_batch, tkx, tc), F32),
                        pltpu.VMEM((tkx, ts), F32), pltpu.VMEM((tkx, ts), F32)]
                       + [pltpu.VMEM((tkx, HEAD), F32)] * 4,
        compiler_params=_params(("parallel", "parallel", "arbitrary")),
        name="seq_dft",
    )(a3d, q3d, c0, s0, jnp.asarray(perm, dtype=BF16))


def _top_k_gates(logits_t):
    n_exp, n_tok = logits_t.shape
    expert = lax.broadcasted_iota(jnp.int32, (n_exp, n_tok), 0)
    slot = lax.broadcasted_iota(jnp.int32, (TOP_K, n_tok), 0)
    work = logits_t
    top = jnp.zeros((TOP_K, n_tok), F32)
    idx = jnp.zeros((TOP_K, n_tok), jnp.int32)
    for c in range(TOP_K):
        best = jnp.max(work, axis=0, keepdims=True)
        where = jnp.min(jnp.where(work == best, expert, n_exp), axis=0, keepdims=True)
        top = jnp.where(slot == c, best, top)
        idx = jnp.where(slot == c, where, idx)
        work = jnp.where(expert == where, -jnp.inf, work)
    e = jnp.exp(top - jnp.max(top, axis=0, keepdims=True))
    return idx, e / jnp.sum(e, axis=0, keepdims=True)


def _mix_out_body(*refs, alpha, d_a, seq_tiles):
    n_req = len(seq_tiles)
    per_req = [refs[4 * q:4 * q + 4] for q in range(n_req)]
    (g0_ref, b0_ref, wo_ref, g1_ref, b1_ref, rwh_ref, rwl_ref, rb_ref,
     x1r_ref, idx_ref, gate_ref) = refs[4 * n_req:]
    req, tile = pl.program_id(0), pl.program_id(1)

    def tile_of_request(x_ref, lo_ref, hi_ref, yb_ref, n_p):
        xln = _layer_norm(x_ref[...], g0_ref[...], b0_ref[...])
        ya = jnp.where(tile % n_p < n_p // 2, lo_ref[...], hi_ref[...])
        mix = _dot(ya, wo_ref[:d_a, :]) + _dot(yb_ref[...], wo_ref[d_a:, :])
        x1 = _layer_norm(alpha * xln + mix, g1_ref[...], b1_ref[...])
        _store_token_rows(x1r_ref, x1)
        hi = x1.astype(BF16)
        lo = (x1 - hi.astype(F32)).astype(BF16)
        logits = (_dot(hi, rwh_ref[...]) + _dot(hi, rwl_ref[...])
                  + _dot(lo, rwh_ref[...]) + rb_ref[...])
        idx_ref[...], gate_ref[...] = _top_k_gates(logits.T)

    for q in range(n_req):
        pl.when(req == q)(functools.partial(tile_of_request, *per_req[q], seq_tiles[q]))


def _mix_out(xs, ya_los, ya_his, ybs, g0, b0, w_o, g1, b1, rw_hi, rw_lo, rb, alpha):
    n_req = len(xs)
    rows, d = xs[0].shape
    assert all(x.shape == (rows, d) for x in xs)
    d_a = ya_los[0].shape[2]
    n_exp = rw_hi.shape[1]
    tm = min([ROW_TILE] + [lo.shape[1] for lo in ya_los])
    tiles = rows // tm
    seq_tiles = tuple(2 * lo.shape[1] // tm for lo in ya_los)
    rpt = d // LANES

    def tile_of(q, req, tile):
        return jnp.where(req == q, tile, jnp.where(req < q, 0, tiles - 1))

    args, in_specs = [], []
    for q in range(n_req):
        n_p = seq_tiles[q]
        rows_spec = lambda w, q=q: pl.BlockSpec(
            (tm, w), lambda req, tile: (tile_of(q, req, tile), 0))
        lo_spec = pl.BlockSpec(
            (None, tm, d_a), lambda req, tile, q=q, n_p=n_p: (
                tile_of(q, req, tile) // n_p,
                jnp.minimum(tile_of(q, req, tile) % n_p, n_p // 2 - 1), 0))
        hi_spec = pl.BlockSpec(
            (None, tm, d_a), lambda req, tile, q=q, n_p=n_p: (
                tile_of(q, req, tile) // n_p,
                jnp.maximum(tile_of(q, req, tile) % n_p - n_p // 2, 0), 0))
        args += [xs[q], ya_los[q], ya_his[q], ybs[q]]
        in_specs += [rows_spec(d), lo_spec, hi_spec, rows_spec(ybs[q].shape[1])]
    vec = _whole((1, d))
    col = pl.BlockSpec((TOP_K, tm), lambda req, tile: (0, req * tiles + tile))
    return pl.pallas_call(
        functools.partial(_mix_out_body, alpha=alpha, d_a=d_a, seq_tiles=seq_tiles),
        grid=(n_req, tiles),
        in_specs=in_specs + [vec, vec, _whole(w_o.shape), vec, vec, _whole(rw_hi.shape),
                             _whole(rw_lo.shape), _whole((1, n_exp))],
        out_specs=[pl.BlockSpec((tm * rpt, LANES), lambda req, tile: (req * tiles + tile, 0)),
                   col, col],
        out_shape=[jax.ShapeDtypeStruct((n_req * rows * rpt, LANES), F32),
                   jax.ShapeDtypeStruct((TOP_K, n_req * rows), jnp.int32),
                   jax.ShapeDtypeStruct((TOP_K, n_req * rows), F32)],
        compiler_params=_params(("parallel", "parallel")),
        name="mix_out",
    )(*args, g0, b0, w_o, g1, b1, rw_hi, rw_lo, rb)


def _column_streams(w2d, tr):
    rows, cols = w2d.shape
    n = PREP_STREAMS if cols % (PREP_STREAMS * SPLIT_GROUP) == 0 else 1
    specs = [pl.BlockSpec((tr, cols // n), functools.partial(lambda i, q: (i, q), q=q))
             for q in range(n)]
    return [w2d] * n, specs


def _split_gu_body(*refs):
    *w_refs, perm_ref, g_ref, l_ref = refs
    half = SPLIT_GROUP // 2
    groups = w_refs[0].shape[1] // SPLIT_GROUP
    for q, w_ref in enumerate(w_refs):
        for c in range(groups):
            w = w_ref[:, c * SPLIT_GROUP:(c + 1) * SPLIT_GROUP].astype(BF16)
            r = _dot(w, perm_ref[...])
            o = (q * groups + c) * half
            g_ref[:, o:o + half] = r[:, :half].astype(BF16)
            l_ref[:, o:o + half] = r[:, half:].astype(BF16)


def _split_gu(w2d):
    rows, cols = w2d.shape
    assert cols % SPLIT_GROUP == 0
    tr = min(PREP_ROWS, rows)
    perm = np.zeros((SPLIT_GROUP, SPLIT_GROUP), np.float32)
    j = np.arange(SPLIT_GROUP // 2)
    perm[2 * j, j] = 1.0
    perm[2 * j + 1, SPLIT_GROUP // 2 + j] = 1.0
    streams, specs = _column_streams(w2d, tr)
    out = jax.ShapeDtypeStruct((rows, cols // 2), BF16)
    return pl.pallas_call(
        _split_gu_body,
        grid=(rows // tr,),
        in_specs=specs + [_whole(perm.shape)],
        out_specs=[pl.BlockSpec((tr, cols // 2), lambda i: (i, 0))] * 2,
        out_shape=[out, out],
        compiler_params=_params(("parallel",)),
        name="split_gu",
    )(*streams, jnp.asarray(perm, dtype=BF16))


def _cast_body(*refs):
    *w_refs, o_ref = refs
    width = w_refs[0].shape[1]
    for q, w_ref in enumerate(w_refs):
        o_ref[:, q * width:(q + 1) * width] = w_ref[...].astype(BF16)


def _cast_bf16(w2d):
    rows, cols = w2d.shape
    tr = min(PREP_ROWS, rows)
    streams, specs = _column_streams(w2d, tr)
    return pl.pallas_call(
        _cast_body,
        grid=(rows // tr,),
        in_specs=specs,
        out_specs=pl.BlockSpec((tr, cols), lambda i: (i, 0)),
        out_shape=jax.ShapeDtypeStruct((rows, cols), BF16),
        compiler_params=_params(("parallel",)),
        name="cast_bf16",
    )(*streams)


def _moe_body(be_ref, nv_ref, src_ref, src_next_ref, dst_ref, dst_prev_ref, x_hbm, wg_hbm,
              wl_hbm, bg_ref, bl_ref, wd_hbm, bd_ref, y_hbm, xbuf_ref, xs_ref, act_ref, obuf_ref,
              wg_ref, wl_ref, wd_ref, gsem, ssem, wsem):
    i, n_valid = pl.program_id(0), nv_ref[0]
    e = be_ref[i]
    e_next = be_ref[jnp.minimum(i + 1, pl.num_programs(0) - 1)]
    first_of_expert = jnp.logical_or(i == 0, be_ref[jnp.maximum(i - 1, 0)] != e)
    last_of_expert = jnp.logical_and(i + 1 < n_valid, e_next != e)

    def up_copies(expert):
        return (pltpu.make_async_copy(wg_hbm.at[expert], wg_ref, wsem.at[0]),
                pltpu.make_async_copy(wl_hbm.at[expert], wl_ref, wsem.at[1]))

    def down_copy(expert):
        return pltpu.make_async_copy(wd_hbm.at[expert], wd_ref, wsem.at[2])

    bm, d = xs_ref.shape
    rpt = d // LANES
    pitch = obuf_ref.shape[0] // bm
    slot = i % 2

    def gather_copy(idx_ref, r, to_slot):
        tok = idx_ref[0, 0, r]
        return pltpu.make_async_copy(x_hbm.at[pl.ds(pl.multiple_of(tok * rpt, rpt), rpt), :],
                                     xbuf_ref.at[to_slot, pl.ds(r * pitch, rpt), :], gsem)

    def scatter_copy(idx_ref, r):
        row = idx_ref[0, 0, r]
        return pltpu.make_async_copy(obuf_ref.at[pl.ds(r * pitch, rpt), :],
                                     y_hbm.at[pl.ds(pl.multiple_of(row * rpt, rpt), rpt), :], ssem)

    def wait_gather(of_slot):
        pltpu.make_async_copy(x_hbm.at[pl.ds(0, bm * rpt), :],
                              xbuf_ref.at[of_slot, pl.ds(0, bm * rpt), :], gsem).wait()

    def wait_scatter():
        pltpu.make_async_copy(obuf_ref.at[pl.ds(0, bm * rpt), :],
                              y_hbm.at[pl.ds(0, bm * rpt), :], ssem).wait()

    @pl.when(i < n_valid)
    def _():
        @pl.when(i == 0)
        def _():
            obuf_ref[...] = jnp.zeros_like(obuf_ref)
            for r in range(bm):
                gather_copy(src_ref, r, 0).start(priority=r % 2)
            for copy in up_copies(e):
                copy.start()

        @pl.when(first_of_expert)
        def _():
            down_copy(e).start()
            for copy in up_copies(e):
                copy.wait()

        wait_gather(slot)
        for c, part in enumerate(_load_token_rows(xbuf_ref.at[slot], bm, rpt, pitch)):
            xs_ref[:, c * LANES:(c + 1) * LANES] = part.astype(BF16)
        x = xs_ref[...]
        d_ff = wg_ref.shape[1]
        tf = min(MOE_FF_TILE, d_ff)
        n_ff = d_ff // tf
        for f in range(n_ff):
            fs = slice(f * tf, (f + 1) * tf)
            glu = jnp.minimum(_dot(x, wg_ref[:, fs]) + bg_ref[0, :, fs], SWIGLU_LIMIT)
            lin = jnp.clip(_dot(x, wl_ref[:, fs]) + bl_ref[0, :, fs],
                           -SWIGLU_LIMIT, SWIGLU_LIMIT)
            for r in range(f * bm // n_ff, (f + 1) * bm // n_ff):
                gather_copy(src_next_ref, r, 1 - slot).start(priority=r % 2)
                scatter_copy(dst_prev_ref, r).start(priority=r % 2)
            act = glu * (1.0 / (1.0 + jnp.exp(-SWIGLU_ALPHA * glu))) * (lin + 1.0)
            act_ref[:, fs] = act.astype(BF16)

        wait_scatter()

        @pl.when(last_of_expert)
        def _():
            for copy in up_copies(e_next):
                copy.start()

        @pl.when(first_of_expert)
        def _():
            down_copy(e).wait()

        tn = min(MOE_OUT_TILE, d)
        for n in range(d // tn):
            ns = slice(n * tn, (n + 1) * tn)
            done = _dot(act_ref[...], wd_ref[:, ns]) + bd_ref[0, :, ns]
            for c in range(tn // LANES):
                obuf_ref[pl.ds(n * (tn // LANES) + c, bm, stride=pitch), :] = (
                    done[:, c * LANES:(c + 1) * LANES])

        @pl.when(i == n_valid - 1)
        def _():
            for r in range(bm):
                scatter_copy(dst_ref, r).start(priority=r % 2)
            wait_gather(1 - slot)
            wait_scatter()


def _moe(blk_e, n_valid, src, dst, x1r, w_g, w_l, b_g, b_l, w_d, b_d, n_out_rows):
    n_blocks, _, bm = src.shape
    n_exp, d, d_ff = w_g.shape
    rpt = d // LANES
    pitch = rpt + MOE_ROW_PAD if rpt % 16 == 0 else rpt
    expert = lambda i, be, nv: (be[i], 0, 0)
    idx_block = lambda index_map: pl.BlockSpec((1, 1, bm), index_map, memory_space=pltpu.SMEM)
    grid_spec = pltpu.PrefetchScalarGridSpec(
        num_scalar_prefetch=2,
        grid=(n_blocks,),
        in_specs=[
            idx_block(lambda i, be, nv: (i, 0, 0)),
            idx_block(lambda i, be, nv: (jnp.minimum(i + 1, n_blocks - 1), 0, 0)),
            idx_block(lambda i, be, nv: (i, 0, 0)),
            idx_block(lambda i, be, nv: (jnp.maximum(i - 1, 0), 0, 0)),
            pl.BlockSpec(memory_space=pl.ANY),
            pl.BlockSpec(memory_space=pl.ANY),
            pl.BlockSpec(memory_space=pl.ANY),
            pl.BlockSpec((1, 1, d_ff), expert),
            pl.BlockSpec((1, 1, d_ff), expert),
            pl.BlockSpec(memory_space=pl.ANY),
            pl.BlockSpec((1, 1, d), expert),
        ],
        out_specs=pl.BlockSpec(memory_space=pl.ANY),
        scratch_shapes=[pltpu.VMEM((2, bm * pitch, LANES), F32), pltpu.VMEM((bm, d), BF16),
                        pltpu.VMEM((bm, d_ff), BF16), pltpu.VMEM((bm * pitch, LANES), F32),
                        pltpu.VMEM((d, d_ff), BF16), pltpu.VMEM((d, d_ff), BF16),
                        pltpu.VMEM((d_ff, d), BF16),
                        pltpu.SemaphoreType.DMA(()), pltpu.SemaphoreType.DMA(()),
                        pltpu.SemaphoreType.DMA((3,))],
    )
    return pl.pallas_call(
        _moe_body,
        grid_spec=grid_spec,
        out_shape=jax.ShapeDtypeStruct((n_out_rows * rpt, LANES), F32),
        compiler_params=_params(("arbitrary",)),
        name="moe",
    )(blk_e, n_valid, src, src, dst, dst, x1r, w_g, w_l, b_g, b_l, w_d, b_d)


def _ln_out_body(x1r_ref, gate_ref, *refs, alpha):
    *y_refs, g_ref, b_ref, o_ref = refs
    tm, d = o_ref.shape
    rpt = d // LANES
    gates = gate_ref[...]
    parts = [alpha * p for p in _load_token_rows(x1r_ref, tm, rpt)]
    for c, y_ref in enumerate(y_refs):
        gate = jnp.broadcast_to(gates[:, c:c + 1], (tm, LANES))
        parts = [p + gate * y for p, y in zip(parts, _load_token_rows(y_ref, tm, rpt))]
    o_ref[...] = _layer_norm(jnp.concatenate(parts, axis=1), g_ref[...], b_ref[...])


def _ln_out(x1r, y_rows, gates, g, b, alpha, rows, row0, total_rows):
    d = g.shape[1]
    rpt = d // LANES
    tm = min(ROW_TILE, rows)
    assert row0 % tm == 0 and total_rows % tm == 0
    tile0 = row0 // tm
    tok = lambda off: pl.BlockSpec((tm * rpt, LANES), lambda i: (off + tile0 + i, 0))
    return pl.pallas_call(
        functools.partial(_ln_out_body, alpha=alpha),
        grid=(rows // tm,),
        in_specs=[tok(0), pl.BlockSpec((tm, TOP_K), lambda i: (tile0 + i, 0))]
                 + [tok(c * (total_rows // tm)) for c in range(TOP_K)]
                 + [_whole((1, d)), _whole((1, d))],
        out_specs=pl.BlockSpec((tm, d), lambda i: (i, 0)),
        out_shape=jax.ShapeDtypeStruct((rows, d), F32),
        compiler_params=_params(("parallel",)),
        name="ln_out",
    )(x1r, gates, *([y_rows] * TOP_K), g, b)


def _route(top_idx, n_exp, bm):
    t = top_idx.shape[0]
    tk = t * TOP_K
    assert n_exp * tk <= 2 ** 31
    packed = jnp.sort(top_idx.reshape(-1) * tk + jnp.arange(tk, dtype=jnp.int32))
    sorted_e, order = packed // tk, packed % tk
    experts = jnp.arange(n_exp, dtype=jnp.int32)
    starts = jnp.sum(sorted_e[None, :] < experts[:, None], axis=1, dtype=jnp.int32)
    counts = jnp.sum(sorted_e[None, :] == experts[:, None], axis=1, dtype=jnp.int32)
    padded = ((counts + bm - 1) // bm) * bm
    pends = jnp.cumsum(padded)
    pstarts = pends - padded
    n_blocks = (tk + n_exp * bm) // bm
    blk_start = jnp.arange(n_blocks, dtype=jnp.int32) * bm
    blk_e = jnp.minimum(jnp.sum(pends[None, :] <= blk_start[:, None], axis=1, dtype=jnp.int32),
                        n_exp - 1)
    n_valid = (pends[-1:] // bm).astype(jnp.int32)
    lane = jnp.arange(bm, dtype=jnp.int32)[None, :]
    row_in_expert = (blk_start - pstarts[blk_e])[:, None] + lane
    valid = row_in_expert < counts[blk_e][:, None]
    pick = order[jnp.clip(starts[blk_e][:, None] + row_in_expert, 0, tk - 1)]
    tok, choice = pick // TOP_K, pick % TOP_K
    src = jnp.where(valid, tok, 0)
    spare = tk + blk_start[:, None] + lane - (starts + counts)[blk_e][:, None]
    dst = jnp.where(valid, choice * t + tok, spare)
    return src[:, None, :], dst[:, None, :], blk_e, n_valid


def _channel_dft_table():
    n = np.arange(HEAD)
    ang = 2.0 * np.pi * ((n[:, None] * n[None, :]) % HEAD) / HEAD
    return jnp.asarray(np.concatenate([np.cos(ang), np.sin(ang)], axis=1), dtype=BF16)


def kernel(x_prompt, x_sample, ln_in_g, ln_in_b, w_in, sgu_norm_g, sgu_norm_b, sgu_w, sgu_b,
           w_o, ln1_g, ln1_b, router_w, router_b, w_gu, b_gu, w_down, b_down, ln2_g, ln2_b):
    depth, d, d_in = w_in.shape
    assert depth == 1, "single-layer encoder only"
    n_heads_b, chunk, _ = sgu_w.shape[1:]
    assert chunk == HEAD
    d_b = n_heads_b * HEAD
    d_a = d_in - 2 * d_b
    n_exp = router_w.shape[2]
    d_ff = w_down.shape[2]
    alpha = (2.0 * depth) ** 0.25
    vec = lambda p: p.reshape(1, -1)

    w_in_b = w_in[0].astype(BF16)
    w_o_b = w_o[0].astype(BF16)
    sgu_w_b = sgu_w[0].astype(BF16)
    sgu_bias = jnp.broadcast_to(sgu_b[0][:, :, None], (n_heads_b, HEAD, HEAD))
    cs = _channel_dft_table()
    rw_hi = router_w[0].astype(BF16)
    rw_lo = (router_w[0] - rw_hi.astype(F32)).astype(BF16)
    w_g, w_l = _split_gu(w_gu[0].reshape(n_exp * d, 2 * d_ff))
    w_g = w_g.reshape(n_exp, d, d_ff)
    w_l = w_l.reshape(n_exp, d, d_ff)
    b_g = b_gu[0][:, None, 0::2]
    b_l = b_gu[0][:, None, 1::2]
    w_d = _cast_bf16(w_down[0].reshape(n_exp * d_ff, d)).reshape(n_exp, d_ff, d)
    b_d = b_down[0][:, None, :]

    requests = (x_prompt, x_sample)
    total_rows = sum(x.shape[0] * x.shape[1] for x in requests)
    xs, ya_los, ya_his, ybs = [], [], [], []
    for x in requests:
        n_batch, seq, _ = x.shape
        x2d = x.reshape(n_batch * seq, d)
        a, q, yb = _mix_in(x2d, vec(ln_in_g), vec(ln_in_b), w_in_b, cs, vec(sgu_norm_g[0]),
                           vec(sgu_norm_b[0]), sgu_w_b, sgu_bias, d_a, d_b)
        ya_lo, ya_hi = _seq_dft(a.reshape(n_batch, seq, d_a), q.reshape(n_batch, seq, d_a))
        xs.append(x2d)
        ya_los.append(ya_lo)
        ya_his.append(ya_hi)
        ybs.append(yb)
    x1r, top_idx, top_gates = _mix_out(xs, ya_los, ya_his, ybs, vec(ln_in_g), vec(ln_in_b),
                                       w_o_b, vec(ln1_g[0]), vec(ln1_b[0]), rw_hi, rw_lo,
                                       vec(router_b[0]), alpha)

    bm = min(MOE_ROWS, total_rows * TOP_K)
    gates = top_gates.T
    src, dst, blk_e, n_valid = _route(top_idx.T, n_exp, bm)
    y_rows = _moe(blk_e, n_valid, src, dst, x1r, w_g, w_l, b_g, b_l, w_d, b_d,
                  total_rows * TOP_K + n_exp * bm)

    outs, row0 = [], 0
    for x in requests:
        rows = x.shape[0] * x.shape[1]
        out = _ln_out(x1r, y_rows, gates, vec(ln2_g[0]), vec(ln2_b[0]), alpha,
                      rows, row0, total_rows)
        outs.append(out.reshape(x.shape))
        row0 += rows
    return tuple(outs)
```
